```python
import math
import jax, jax.numpy as jnp
from jax import lax
import numpy as np


D_MODEL = 2048
BATCH = 2
SEQ = 4096
DEPTH = 4
DEC_BATCH = 8
DEC_SEQ = 1
PAST_LEN = 16384
PAGE_SIZE = 128

N_EVEN = (DEPTH + 1) // 2
N_ODD = DEPTH // 2

R_HEADS = 4
R_DK = D_MODEL // 8
R_DV = D_MODEL // 8
R_WIDTH = R_HEADS * R_DV
R_CHUNK = 128
CM_GROUPS = 4
CM_CHUNK = 128
CM_WIDTH = D_MODEL // 2
CM_GDIM = CM_WIDTH // CM_GROUPS
EVEN_IN = 2 * R_HEADS * R_DK + 2 * R_WIDTH + 2 * CM_WIDTH
EVEN_OUT = R_WIDTH + CM_WIDTH
DA_HEADS = 8
DA_DH = D_MODEL // (2 * DA_HEADS)
DA_QK_HEADS = 2 * DA_HEADS
DA_VDIM = 2 * DA_DH
ODD_IN = 3 * D_MODEL
ODD_OUT = DA_HEADS * DA_VDIM
Q_BLOCK = 128
ROPE_THETA = 10000.0
D_FF = ((8 * D_MODEL // 3 + 255) // 256) * 256
EPS = 1e-6

kernel_name = "hybrid_retention_gmlp_diffattn_macaron_step"

F32 = jnp.float32


def rms_norm(x, g, eps=EPS):
    x32 = x.astype(F32)
    y = x32 * lax.rsqrt(jnp.mean(x32 * x32, axis=-1, keepdims=True) + eps)
    if g is not None:
        y = y * g.astype(F32)
    return y.astype(x.dtype)


def layer_norm(x, g, b, eps=1e-5):
    x32 = x.astype(F32)
    mu = jnp.mean(x32, axis=-1, keepdims=True)
    xc = x32 - mu
    y = xc * lax.rsqrt(jnp.mean(xc * xc, axis=-1, keepdims=True) + eps)
    return (y * g.astype(F32) + b.astype(F32)).astype(x.dtype)


def rope(x, pos):
    d = x.shape[-1]
    inv = ROPE_THETA ** (-jnp.arange(0, d, 2, dtype=F32) / d)
    ang = pos.astype(F32)[:, None] * inv[None, :]
    cos = jnp.cos(ang)[None, :, None, :]
    sin = jnp.sin(ang)[None, :, None, :]
    x32 = x.astype(F32)
    x1, x2 = x32[..., : d // 2], x32[..., d // 2:]
    return jnp.concatenate([x1 * cos - x2 * sin, x2 * cos + x1 * sin], axis=-1).astype(x.dtype)


def swiglu(x, w1, w3, w2):
    return (jax.nn.silu(x @ w1) * (x @ w3)) @ w2


def retention_log_decay():
    expo = -5.0 - 7.0 * jnp.arange(R_HEADS, dtype=F32) / max(R_HEADS - 1, 1)
    return jnp.log1p(-jnp.exp2(expo))


def retention_chunk(q, k, v, state, log_g):
    L = q.shape[2]
    idx = jnp.arange(L, dtype=F32)
    diff = idx[:, None] - idx[None, :]
    decay = jnp.where(diff >= 0, jnp.exp(log_g[:, None, None] * jnp.maximum(diff, 0.0)), 0.0)
    s = jnp.einsum('bhid,bhjd->bhij', q, k) * decay
    inner = jnp.einsum('bhij,bhje->bhie', s, v)
    cross = jnp.einsum('bhid,bhde->bhie', q, state) * jnp.exp(log_g[:, None] * (idx + 1.0))[:, :, None]
    k_dec = k * jnp.exp(log_g[:, None] * (L - 1.0 - idx))[:, :, None]
    new_state = state * jnp.exp(log_g * L)[:, None, None] + jnp.einsum('bhjd,bhje->bhde', k_dec, v)
    return inner + cross, new_state


def retention_prompt(q, k, v, log_g):
    B, H, S, _ = q.shape
    nc = S // R_CHUNK

    def to_chunks(t):
        return t.reshape(B, H, nc, R_CHUNK, t.shape[-1]).transpose(2, 0, 1, 3, 4)

    def step(state, inp):
        qc, kc, vc = inp
        o, state = retention_chunk(qc, kc, vc, state, log_g)
        return state, o

    state0 = jnp.zeros((B, H, R_DK, R_DV), F32)
    state, o = lax.scan(step, state0, (to_chunks(q), to_chunks(k), to_chunks(v)))
    o = o.transpose(1, 2, 0, 3, 4).reshape(B, H, S, R_DV)
    return o, state


def chunk_mlp(u, vn, ws, bs):
    B, T, _ = u.shape
    L = min(T, CM_CHUNK)
    nc = T // L
    w = jnp.tril(ws[:, :L, :L])
    vg = vn.reshape(B, nc, L, CM_GROUPS, CM_GDIM)
    mixed = jnp.einsum('gij,bcjgd->bcigd', w, vg) + bs[:, :L].T[None, None, :, :, None]
    return u * mixed.reshape(B, T, CM_WIDTH)


def even_mixer(h, w_in, w_out, ws, bs, ln_g, ln_b, pos, ret_state):
    B, T, _ = h.shape
    qk_w = R_HEADS * R_DK
    cuts = [qk_w, 2 * qk_w, 2 * qk_w + R_WIDTH, 2 * qk_w + 2 * R_WIDTH, 2 * qk_w + 2 * R_WIDTH + CM_WIDTH]
    q, k, v, g, u, vc = jnp.split(h @ w_in, cuts, axis=-1)
    q = rope(q.reshape(B, T, R_HEADS, R_DK), pos).astype(F32).transpose(0, 2, 1, 3)
    k = (rope(k.reshape(B, T, R_HEADS, R_DK), pos).astype(F32) * (R_DK ** -0.5)).transpose(0, 2, 1, 3)
    v = v.reshape(B, T, R_HEADS, R_DV).astype(F32).transpose(0, 2, 1, 3)
    log_g = retention_log_decay()
    if ret_state is None:
        o, st = retention_prompt(q, k, v, log_g)
    else:
        o, st = retention_chunk(q, k, v, ret_state.astype(F32), log_g)
    o = rms_norm(o.transpose(0, 2, 1, 3), None).reshape(B, T, R_WIDTH)
    ret_out = jax.nn.silu(g) * o.astype(h.dtype)
    vn = layer_norm(jax.nn.gelu(vc), ln_g, ln_b)
    cm_out = chunk_mlp(jax.nn.gelu(u), vn, ws, bs)
    out = jnp.concatenate([ret_out, cm_out], axis=-1) @ w_out
    return out, st, vn


def diff_project(h, w_in, qn, kn, pos):
    B, T, _ = h.shape
    q, k, v = jnp.split(h @ w_in, 3, axis=-1)
    q = rope(rms_norm(q.reshape(B, T, DA_QK_HEADS, DA_DH), qn), pos)
    k = rope(rms_norm(k.reshape(B, T, DA_QK_HEADS, DA_DH), kn), pos)
    v = v.reshape(B, T, DA_HEADS, DA_VDIM)
    return q, k, v


def diff_lambda(lq1, lk1, lq2, lk2, lam_init):
    return (jnp.exp(jnp.sum(lq1.astype(F32) * lk1.astype(F32)))
            - jnp.exp(jnp.sum(lq2.astype(F32) * lk2.astype(F32))) + lam_init)


def diff_weights(s, lam):
    p = jax.nn.softmax(s, axis=-1)
    B, _, Tq, S = p.shape
    p = p.reshape(B, DA_HEADS, 2, Tq, S)
    return p[:, :, 0] - lam * p[:, :, 1]


def diff_attn_prompt(q, k, v, lam):
    B, S = q.shape[:2]
    nqb = S // Q_BLOCK
    kpos = jnp.arange(S)
    scale = DA_DH ** -0.5

    def block(i):
        qb = lax.dynamic_slice_in_dim(q, i * Q_BLOCK, Q_BLOCK, axis=1)
        s = jnp.einsum('bqhd,bkhd->bhqk', qb, k, preferred_element_type=F32) * scale
        qpos = i * Q_BLOCK + jnp.arange(Q_BLOCK)
        s = jnp.where(kpos[None, :] <= qpos[:, None], s, -jnp.inf)
        w = diff_weights(s, lam)
        return jnp.einsum('bhqk,bkhe->bqhe', w.astype(v.dtype), v)

    o = lax.map(block, jnp.arange(nqb))
    return o.transpose(1, 0, 2, 3, 4).reshape(B, S, DA_HEADS, DA_VDIM)


def diff_attn_sample(q, k, v, k_pool, v_pool, layer, page_table, lam):
    Bd, T = q.shape[:2]
    kp = k_pool[layer, page_table].reshape(Bd, -1, DA_QK_HEADS, DA_DH)
    vp = v_pool[layer, page_table].reshape(Bd, -1, DA_HEADS, DA_VDIM)
    past = kp.shape[1]
    scale = DA_DH ** -0.5
    s_past = jnp.einsum('bqhd,bkhd->bhqk', q, kp, preferred_element_type=F32) * scale
    s_new = jnp.einsum('bqhd,bkhd->bhqk', q, k, preferred_element_type=F32) * scale
    tpos = jnp.arange(T)
    s_new = jnp.where(tpos[None, :] <= tpos[:, None], s_new, -jnp.inf)
    w = diff_weights(jnp.concatenate([s_past, s_new], axis=-1), lam)
    return (jnp.einsum('bhqk,bkhe->bqhe', w[..., :past].astype(v.dtype), vp)
            + jnp.einsum('bhqk,bkhe->bqhe', w[..., past:].astype(v.dtype), v))


def diff_out(o, subln_g, lam_init, w_out):
    B, T = o.shape[:2]
    o = rms_norm(o, subln_g, eps=1e-5) * (1.0 - lam_init)
    return o.reshape(B, T, ODD_OUT) @ w_out


def setup_inputs(seed: int = 0) -> dict:
    key = jax.random.key(seed)
    ks = jax.random.split(key, 32)

    def nrm(i, shape, scale):
        return jax.random.normal(ks[i], shape, F32) * scale

    n_pages = PAST_LEN // PAGE_SIZE
    n_pool = (DEC_BATCH * n_pages * 5) // 4
    page_table = jax.random.permutation(ks[5], n_pool)[: DEC_BATCH * n_pages].reshape(DEC_BATCH, n_pages).astype(jnp.int32)
    d_in = D_MODEL ** -0.5
    return {
        "x_prompt": nrm(0, (BATCH, SEQ, D_MODEL), 1.0),
        "x_sample": nrm(1, (DEC_BATCH, DEC_SEQ, D_MODEL), 1.0),
        "state_ret": nrm(2, (N_EVEN, DEC_BATCH, R_HEADS, R_DK, R_DV), 1.0),
        "cache_k": nrm(3, (N_ODD, n_pool, PAGE_SIZE, DA_QK_HEADS, DA_DH), 1.0),
        "cache_v": nrm(4, (N_ODD, n_pool, PAGE_SIZE, DA_HEADS, DA_VDIM), 1.0),
        "page_table": page_table,
        "ffn_a_norm": 1.0 + nrm(6, (DEPTH, D_MODEL), 0.02),
        "ffn_a_w1": nrm(7, (DEPTH, D_MODEL, D_FF), d_in),
        "ffn_a_w3": nrm(8, (DEPTH, D_MODEL, D_FF), d_in),
        "ffn_a_w2": nrm(9, (DEPTH, D_FF, D_MODEL), D_FF ** -0.5),
        "mix_norm": 1.0 + nrm(10, (DEPTH, D_MODEL), 0.02),
        "ffn_b_norm": 1.0 + nrm(11, (DEPTH, D_MODEL), 0.02),
        "ffn_b_w1": nrm(12, (DEPTH, D_MODEL, D_FF), d_in),
        "ffn_b_w3": nrm(13, (DEPTH, D_MODEL, D_FF), d_in),
        "ffn_b_w2": nrm(14, (DEPTH, D_FF, D_MODEL), D_FF ** -0.5),
        "ev_w_in": nrm(15, (N_EVEN, D_MODEL, EVEN_IN), d_in),
        "ev_w_out": nrm(16, (N_EVEN, EVEN_OUT, D_MODEL), EVEN_OUT ** -0.5),
        "cm_ws": nrm(17, (N_EVEN, CM_GROUPS, CM_CHUNK, CM_CHUNK), CM_CHUNK ** -0.5),
        "cm_bs": 1.0 + nrm(18, (N_EVEN, CM_GROUPS, CM_CHUNK), 0.02),
        "cm_ln_g": 1.0 + nrm(19, (N_EVEN, CM_WIDTH), 0.02),
        "cm_ln_b": nrm(20, (N_EVEN, CM_WIDTH), 0.02),
        "od_w_in": nrm(21, (N_ODD, D_MODEL, ODD_IN), d_in),
        "od_w_out": nrm(22, (N_ODD, ODD_OUT, D_MODEL), ODD_OUT ** -0.5),
        "da_q_norm": 1.0 + nrm(23, (N_ODD, DA_DH), 0.02),
        "da_k_norm": 1.0 + nrm(24, (N_ODD, DA_DH), 0.02),
        "da_lam_q1": nrm(25, (N_ODD, DA_DH), 0.1),
        "da_lam_k1": nrm(26, (N_ODD, DA_DH), 0.1),
        "da_lam_q2": nrm(27, (N_ODD, DA_DH), 0.1),
        "da_lam_k2": nrm(28, (N_ODD, DA_DH), 0.1),
        "da_subln": 1.0 + nrm(29, (N_ODD, DA_VDIM), 0.02),
    }


def reference(x_prompt, x_sample, state_ret, cache_k, cache_v, page_table,
              ffn_a_norm, ffn_a_w1, ffn_a_w3, ffn_a_w2, mix_norm,
              ffn_b_norm, ffn_b_w1, ffn_b_w3, ffn_b_w2,
              ev_w_in, ev_w_out, cm_ws, cm_bs, cm_ln_g, cm_ln_b,
              od_w_in, od_w_out, da_q_norm, da_k_norm,
              da_lam_q1, da_lam_k1, da_lam_q2, da_lam_k2, da_subln):
    S = x_prompt.shape[1]
    T = x_sample.shape[1]
    pos_p = jnp.arange(S, dtype=jnp.int32)
    pos_s = PAST_LEN + jnp.arange(T, dtype=jnp.int32)
    xp, xs = x_prompt, x_sample
    ret_p, ret_s, cmv_p, cmv_s = [], [], [], []
    kp_l, vp_l, ks_l, vs_l = [], [], [], []
    for l in range(DEPTH):
        xp = xp + 0.5 * swiglu(rms_norm(xp, ffn_a_norm[l]), ffn_a_w1[l], ffn_a_w3[l], ffn_a_w2[l])
        xs = xs + 0.5 * swiglu(rms_norm(xs, ffn_a_norm[l]), ffn_a_w1[l], ffn_a_w3[l], ffn_a_w2[l])
        hp = rms_norm(xp, mix_norm[l])
        hs = rms_norm(xs, mix_norm[l])
        if l % 2 == 0:
            e = l // 2
            op, stp, vnp = even_mixer(hp, ev_w_in[e], ev_w_out[e], cm_ws[e], cm_bs[e], cm_ln_g[e], cm_ln_b[e], pos_p, None)
            os_, sts, vns = even_mixer(hs, ev_w_in[e], ev_w_out[e], cm_ws[e], cm_bs[e], cm_ln_g[e], cm_ln_b[e], pos_s, state_ret[e])
            ret_p.append(stp)
            ret_s.append(sts)
            cmv_p.append(vnp[:, -CM_CHUNK:])
            cmv_s.append(vns)
        else:
            o = l // 2
            lam_init = 0.8 - 0.6 * math.exp(-0.3 * l)
            lam = diff_lambda(da_lam_q1[o], da_lam_k1[o], da_lam_q2[o], da_lam_k2[o], lam_init)
            qp, kp, vp = diff_project(hp, od_w_in[o], da_q_norm[o], da_k_norm[o], pos_p)
            op = diff_out(diff_attn_prompt(qp, kp, vp, lam), da_subln[o], lam_init, od_w_out[o])
            qs, ks_, vs_ = diff_project(hs, od_w_in[o], da_q_norm[o], da_k_norm[o], pos_s)
            os_ = diff_out(diff_attn_sample(qs, ks_, vs_, cache_k, cache_v, o, page_table, lam), da_subln[o], lam_init, od_w_out[o])
            kp_l.append(kp)
            vp_l.append(vp)
            ks_l.append(ks_)
            vs_l.append(vs_)
        xp = xp + op
        xs = xs + os_
        xp = xp + 0.5 * swiglu(rms_norm(xp, ffn_b_norm[l]), ffn_b_w1[l], ffn_b_w3[l], ffn_b_w2[l])
        xs = xs + 0.5 * swiglu(rms_norm(xs, ffn_b_norm[l]), ffn_b_w1[l], ffn_b_w3[l], ffn_b_w2[l])
    return (xp, xs, jnp.stack(ret_p), jnp.stack(ret_s), jnp.stack(cmv_p), jnp.stack(cmv_s),
            jnp.stack(kp_l), jnp.stack(vp_l), jnp.stack(ks_l), jnp.stack(vs_l))
```

```python
import functools
import math

import numpy as np
import jax
import jax.numpy as jnp
from jax import lax
from jax.experimental import pallas as pl
from jax.experimental.pallas import tpu as pltpu

F32 = jnp.float32
BF16 = jnp.bfloat16

PAST_LEN = 16384
R_HEADS = 4
R_DK = 256
R_DV = 256
R_CHUNK = 128
CM_GROUPS = 4
CM_GDIM = 256
DA_HEADS = 8
DA_DH = 128
DA_VDIM = 256
LOG2_DA_DH = 7
LOG2_DA_VDIM = 8
ROPE_THETA = 10000.0
EPS = 1e-6
SUBLN_EPS = 1e-5
LN_EPS = 1e-5

VMEM_LIMIT_BYTES = 56 * 1024 * 1024


def _cparams(n_axes):
    return pltpu.CompilerParams(
        dimension_semantics=("arbitrary",) * n_axes,
        vmem_limit_bytes=VMEM_LIMIT_BYTES,
    )


def _silu(a):
    return a / (1.0 + jnp.exp(-a))


def _gelu_tanh(x):
    c = math.sqrt(2.0 / math.pi)
    return x * (0.5 * (1.0 + jnp.tanh(c * (x + 0.044715 * (x * x * x)))))


def _rms(x, eps):
    return x * lax.rsqrt(jnp.mean(x * x, axis=-1, keepdims=True) + eps)


def _dot(a, b):
    return jnp.dot(a, b, preferred_element_type=F32)


def _dot_nt(a, b):
    return lax.dot_general(a, b, (((1,), (1,)), ((), ())), preferred_element_type=F32)


def _ffn_body(x_ref, g_ref, w1_ref, w3_ref, w2_ref, o_ref, xn_ref, *, n_split):
    j = pl.program_id(1)

    @pl.when(j == 0)
    def _init():
        x = x_ref[...]
        xn_ref[...] = (_rms(x, EPS) * g_ref[...]).astype(BF16)
        o_ref[...] = x

    xn = xn_ref[...]
    a = _dot(xn, w1_ref[...].astype(BF16))
    b = _dot(xn, w3_ref[...].astype(BF16))
    h = ((_silu(a) * b) * 0.5).astype(BF16)
    d = o_ref.shape[-1]
    w = d // n_split
    for n in range(n_split):
        sl = slice(n * w, (n + 1) * w)
        o_ref[:, sl] += _dot(h, w2_ref[:, sl].astype(BF16))


def _ffn(x, norm3, w1, w3, w2, layer, bm, bf, single_buffer_x):
    m, d = x.shape
    ff = w1.shape[-1]
    x_mode = pl.Buffered(1) if single_buffer_x else None
    return pl.pallas_call(
        functools.partial(_ffn_body, n_split=4),
        grid=(m // bm, ff // bf),
        in_specs=[
            pl.BlockSpec((bm, d), lambda i, j: (i, 0), pipeline_mode=x_mode),
            pl.BlockSpec((None, 1, d), lambda i, j: (layer, 0, 0)),
            pl.BlockSpec((None, d, bf), lambda i, j: (layer, 0, j)),
            pl.BlockSpec((None, d, bf), lambda i, j: (layer, 0, j)),
            pl.BlockSpec((None, bf, d), lambda i, j: (layer, j, 0)),
        ],
        out_specs=pl.BlockSpec((bm, d), lambda i, j: (i, 0)),
        out_shape=jax.ShapeDtypeStruct((m, d), F32),
        scratch_shapes=[pltpu.VMEM((bm, d), BF16)],
        compiler_params=_cparams(2),
        name="ffn",
    )(x, norm3, w1, w3, w2)


def _normproj_body(x_ref, g_ref, w_ref, o_ref, xn_ref):
    @pl.when(pl.program_id(1) == 0)
    def _init():
        xn_ref[...] = (_rms(x_ref[...], EPS) * g_ref[...]).astype(BF16)

    o_ref[...] = _dot(xn_ref[...], w_ref[...].astype(BF16)).astype(o_ref.dtype)


def _normproj_qk_body(x_ref, g_ref, w_ref, hn_ref, c_ref, s_ref, o_ref, xn_ref):
    @pl.when(pl.program_id(1) == 0)
    def _init():
        xn_ref[...] = (_rms(x_ref[...], EPS) * g_ref[...]).astype(BF16)

    y = _dot(xn_ref[...], w_ref[...].astype(BF16))
    hn = hn_ref[...]
    c = c_ref[...]
    s = s_ref[...]
    for t in range(y.shape[-1] // DA_DH):
        sl = slice(t * DA_DH, (t + 1) * DA_DH)
        yn = _rms(y[:, sl], EPS) * hn
        o_ref[:, sl] = (yn * c + pltpu.roll(yn, DA_DH // 2, axis=1) * s).astype(o_ref.dtype)


def _normproj(x, norm3, norm_layer, w, layer, col0, ncols, bm, bn, out_dtype, qk=None):
    m, d = x.shape
    jb0 = col0 // bn
    in_specs = [
        pl.BlockSpec((bm, d), lambda i, j: (i, 0)),
        pl.BlockSpec((None, 1, d), lambda i, j: (norm_layer, 0, 0)),
        pl.BlockSpec((None, d, bn), lambda i, j: (layer, 0, jb0 + j)),
    ]
    args = [x, norm3, w]
    body = _normproj_body
    if qk is not None:
        hn, c2, s2, table_blocks = qk
        body = _normproj_qk_body
        in_specs += [
            pl.BlockSpec((1, DA_DH), lambda i, j: (0, 0)),
            pl.BlockSpec((bm, DA_DH), lambda i, j: (i % table_blocks, 0)),
            pl.BlockSpec((bm, DA_DH), lambda i, j: (i % table_blocks, 0)),
        ]
        args += [hn, c2, s2]
    return pl.pallas_call(
        body,
        grid=(m // bm, ncols // bn),
        in_specs=in_specs,
        out_specs=pl.BlockSpec((bm, bn), lambda i, j: (i, j)),
        out_shape=jax.ShapeDtypeStruct((m, ncols), out_dtype),
        scratch_shapes=[pltpu.VMEM((bm, d), BF16)],
        compiler_params=_cparams(2),
        name="normproj_qk" if qk is not None else "normproj",
    )(*args)


def _matres_body(a_ref, w_ref, r_ref, o_ref):
    o_ref[...] = r_ref[...] + _dot(a_ref[...].astype(BF16), w_ref[...].astype(BF16))


def _matres(a, w, res, layer, bm, bn):
    m, k = a.shape
    n = w.shape[-1]
    return pl.pallas_call(
        _matres_body,
        grid=(m // bm, n // bn),
        in_specs=[
            pl.BlockSpec((bm, k), lambda i, j: (i, 0)),
            pl.BlockSpec((None, k, bn), lambda i, j: (layer, 0, j)),
            pl.BlockSpec((bm, bn), lambda i, j: (i, j)),
        ],
        out_specs=pl.BlockSpec((bm, bn), lambda i, j: (i, j)),
        out_shape=jax.ShapeDtypeStruct((m, n), F32),
        compiler_params=_cparams(2),
        name="matres",
    )(a, w, res)


def _retention_tables():
    expo = -5.0 - 7.0 * np.arange(R_HEADS, dtype=np.float64) / max(R_HEADS - 1, 1)
    log_g = np.log1p(-np.exp2(expo))
    idx = np.arange(R_CHUNK, dtype=np.float64)
    diff = idx[:, None] - idx[None, :]
    decay = np.where(diff >= 0, np.exp(log_g[:, None, None] * np.maximum(diff, 0.0)), 0.0)
    row_decay = np.exp(log_g[:, None] * (idx + 1.0))[:, :, None]
    k_decay = np.exp(log_g[:, None] * (R_CHUNK - 1.0 - idx))[:, :, None]
    state_decay = tuple(float(v) for v in np.exp(log_g * R_CHUNK))
    gamma = tuple(float(v) for v in np.exp(log_g))
    return (decay.astype(np.float32), row_decay.astype(np.float32), k_decay.astype(np.float32),
            state_decay, gamma)


def _rope_half(x, cos, sin):
    half = x.shape[-1] // 2
    x1 = x[:, :half]
    x2 = x[:, half:]
    return jnp.concatenate([x1 * cos - x2 * sin, x2 * cos + x1 * sin], axis=-1)


def _layer_norm(x, g, b):
    xc = x - jnp.mean(x, axis=-1, keepdims=True)
    return xc * lax.rsqrt(jnp.mean(xc * xc, axis=-1, keepdims=True) + LN_EPS) * g + b


def _even_prompt_body(p_ref, cos_ref, sin_ref, dec_ref, rd_ref, kd_ref, ws_ref, bs_ref, lng_ref, lnb_ref,
                      o_ref, st_ref, vn_ref, state, *, state_decay):
    c = pl.program_id(1)
    last = pl.num_programs(1) - 1

    @pl.when(c == 0)
    def _zero():
        state[...] = jnp.zeros_like(state)

    cos = cos_ref[...]
    sin = sin_ref[...]
    qk_w = R_HEADS * R_DK
    for h in range(R_HEADS):
        q = _rope_half(p_ref[:, h * R_DK:(h + 1) * R_DK], cos, sin)
        k = _rope_half(p_ref[:, qk_w + h * R_DK:qk_w + (h + 1) * R_DK], cos, sin) * (R_DK ** -0.5)
        v = p_ref[:, 2 * qk_w + h * R_DV:2 * qk_w + (h + 1) * R_DV].astype(BF16)
        g = p_ref[:, 3 * qk_w + h * R_DV:3 * qk_w + (h + 1) * R_DV]
        qb = q.astype(BF16)
        s = _dot_nt(qb, k.astype(BF16)) * dec_ref[h]
        st = state[h]
        o = _dot(s.astype(BF16), v) + _dot(qb, st.astype(BF16)) * rd_ref[h]
        kd_t = (k * kd_ref[h]).T.astype(BF16)
        state[h] = st * state_decay[h] + _dot(kd_t, v)
        o_ref[:, h * R_DV:(h + 1) * R_DV] = (_silu(g) * _rms(o, EPS)).astype(o_ref.dtype)

    u0 = 3 * qk_w + R_HEADS * R_DV
    cmw = CM_GROUPS * CM_GDIM
    vn = _layer_norm(_gelu_tanh(p_ref[:, u0 + cmw:u0 + 2 * cmw]), lng_ref[...], lnb_ref[...])
    row = lax.broadcasted_iota(jnp.int32, (R_CHUNK, R_CHUNK), 0)
    col = lax.broadcasted_iota(jnp.int32, (R_CHUNK, R_CHUNK), 1)
    for gi in range(CM_GROUPS):
        sl = slice(gi * CM_GDIM, (gi + 1) * CM_GDIM)
        w = jnp.where(row >= col, ws_ref[gi], 0.0).astype(BF16)
        mixed = _dot(w, vn[:, sl].astype(BF16)) + bs_ref[gi]
        u = p_ref[:, u0 + gi * CM_GDIM:u0 + (gi + 1) * CM_GDIM]
        o_ref[:, R_HEADS * R_DV + gi * CM_GDIM:R_HEADS * R_DV + (gi + 1) * CM_GDIM] = (
            _gelu_tanh(u) * mixed).astype(o_ref.dtype)

    @pl.when(c == last)
    def _emit():
        st_ref[...] = state[...]
        vn_ref[...] = vn


def _even_prompt(proj, cos, sin, tables, ws, bs4, lng3, lnb3, e, batch, seq):
    decay, row_decay, k_decay, state_decay, _ = tables
    nc = seq // R_CHUNK
    width = proj.shape[-1]
    cmw = CM_GROUPS * CM_GDIM
    out_w = R_HEADS * R_DV + cmw
    const3 = lambda b, c: (0, 0, 0)
    return pl.pallas_call(
        functools.partial(_even_prompt_body, state_decay=state_decay),
        grid=(batch, nc),
        in_specs=[
            pl.BlockSpec((R_CHUNK, width), lambda b, c: (b * nc + c, 0)),
            pl.BlockSpec((R_CHUNK, R_DK // 2), lambda b, c: (c, 0)),
            pl.BlockSpec((R_CHUNK, R_DK // 2), lambda b, c: (c, 0)),
            pl.BlockSpec(decay.shape, const3),
            pl.BlockSpec(row_decay.shape, const3),
            pl.BlockSpec(k_decay.shape, const3),
            pl.BlockSpec((None, CM_GROUPS, R_CHUNK, R_CHUNK), lambda b, c: (e, 0, 0, 0)),
            pl.BlockSpec((None, CM_GROUPS, R_CHUNK, 1), lambda b, c: (e, 0, 0, 0)),
            pl.BlockSpec((None, 1, cmw), lambda b, c: (e, 0, 0)),
            pl.BlockSpec((None, 1, cmw), lambda b, c: (e, 0, 0)),
        ],
        out_specs=[
            pl.BlockSpec((R_CHUNK, out_w), lambda b, c: (b * nc + c, 0)),
            pl.BlockSpec((None, R_HEADS, R_DK, R_DV), lambda b, c: (b, 0, 0, 0)),
            pl.BlockSpec((None, R_CHUNK, cmw), lambda b, c: (b, 0, 0)),
        ],
        out_shape=[
            jax.ShapeDtypeStruct((batch * seq, out_w), BF16),
            jax.ShapeDtypeStruct((batch, R_HEADS, R_DK, R_DV), F32),
            jax.ShapeDtypeStruct((batch, R_CHUNK, cmw), F32),
        ],
        scratch_shapes=[pltpu.VMEM((R_HEADS, R_DK, R_DV), F32)],
        compiler_params=_cparams(2),
        name="even_prompt",
    )(proj, cos, sin, jnp.asarray(decay), jnp.asarray(row_decay), jnp.asarray(k_decay), ws, bs4, lng3, lnb3)


def _even_sample_body(p_ref, cos_ref, sin_ref, st_in_ref, ws_ref, bs_ref, lng_ref, lnb_ref,
                      o_ref, st_ref, vn_ref, *, gamma):
    cos = cos_ref[...]
    sin = sin_ref[...]
    qk_w = R_HEADS * R_DK
    r0 = lax.broadcasted_iota(jnp.int32, (R_DK, R_DK), 0)
    r1 = lax.broadcasted_iota(jnp.int32, (R_DK, R_DK), 1)
    eye = jnp.where(r0 == r1, 1.0, 0.0).astype(BF16)
    for h in range(R_HEADS):
        q = _rope_half(p_ref[:, h * R_DK:(h + 1) * R_DK], cos, sin)
        k = _rope_half(p_ref[:, qk_w + h * R_DK:qk_w + (h + 1) * R_DK], cos, sin) * (R_DK ** -0.5)
        v = p_ref[:, 2 * qk_w + h * R_DV:2 * qk_w + (h + 1) * R_DV]
        g = p_ref[:, 3 * qk_w + h * R_DV:3 * qk_w + (h + 1) * R_DV]
        st = st_in_ref[h]
        s = jnp.sum(q * k, axis=-1, keepdims=True)
        qb = jnp.broadcast_to(q, (8, R_DK)).astype(BF16)
        cross = _dot(qb, st.astype(BF16))[0:1, :]
        o = s * v + cross * gamma[h]
        kb = jnp.broadcast_to(k, (128, R_DK)).astype(BF16)
        kcol = _dot_nt(eye, kb)
        kcol = jnp.concatenate([kcol] * (R_DV // 128), axis=-1)
        vb = v.astype(BF16).astype(F32)
        st_ref[h] = st * gamma[h] + kcol * vb
        o_ref[:, h * R_DV:(h + 1) * R_DV] = (_silu(g) * _rms(o, EPS)).astype(o_ref.dtype)

    u0 = 3 * qk_w + R_HEADS * R_DV
    cmw = CM_GROUPS * CM_GDIM
    vn = _layer_norm(_gelu_tanh(p_ref[:, u0 + cmw:u0 + 2 * cmw]), lng_ref[...], lnb_ref[...])
    vn_ref[...] = vn
    for gi in range(CM_GROUPS):
        sl = slice(gi * CM_GDIM, (gi + 1) * CM_GDIM)
        mixed = ws_ref[gi, 0:1, 0:1] * vn[:, sl] + bs_ref[gi, 0:1, :]
        u = p_ref[:, u0 + gi * CM_GDIM:u0 + (gi + 1) * CM_GDIM]
        o_ref[:, R_HEADS * R_DV + gi * CM_GDIM:R_HEADS * R_DV + (gi + 1) * CM_GDIM] = (
            _gelu_tanh(u) * mixed).astype(o_ref.dtype)


def _even_sample(proj3, cos, sin, tables, st_in, ws, bs4, lng3, lnb3, e):
    gamma = tables[4]
    nb = proj3.shape[0]
    width = proj3.shape[-1]
    cmw = CM_GROUPS * CM_GDIM
    out_w = R_HEADS * R_DV + cmw
    return pl.pallas_call(
        functools.partial(_even_sample_body, gamma=gamma),
        grid=(nb,),
        in_specs=[
            pl.BlockSpec((None, 1, width), lambda b: (b, 0, 0)),
            pl.BlockSpec((1, R_DK // 2), lambda b: (0, 0)),
            pl.BlockSpec((1, R_DK // 2), lambda b: (0, 0)),
            pl.BlockSpec((None, None, R_HEADS, R_DK, R_DV), lambda b: (e, b, 0, 0, 0)),
            pl.BlockSpec((None, CM_GROUPS, R_CHUNK, R_CHUNK), lambda b: (e, 0, 0, 0)),
            pl.BlockSpec((None, CM_GROUPS, R_CHUNK, 1), lambda b: (e, 0, 0, 0)),
            pl.BlockSpec((None, 1, cmw), lambda b: (e, 0, 0)),
            pl.BlockSpec((None, 1, cmw), lambda b: (e, 0, 0)),
        ],
        out_specs=[
            pl.BlockSpec((None, 1, out_w), lambda b: (b, 0, 0)),
            pl.BlockSpec((None, R_HEADS, R_DK, R_DV), lambda b: (b, 0, 0, 0)),
            pl.BlockSpec((None, 1, cmw), lambda b: (b, 0, 0)),
        ],
        out_shape=[
            jax.ShapeDtypeStruct((nb, 1, out_w), F32),
            jax.ShapeDtypeStruct((nb, R_HEADS, R_DK, R_DV), F32),
            jax.ShapeDtypeStruct((nb, 1, cmw), F32),
        ],
        compiler_params=_cparams(1),
        name="even_sample",
    )(proj3, cos, sin, st_in, ws, bs4, lng3, lnb3)


def _diff_lambda(lq1_ref, lk1_ref, lq2_ref, lk2_ref, lam_init):
    a = jnp.sum(lq1_ref[...] * lk1_ref[...], axis=-1, keepdims=True)
    b = jnp.sum(lq2_ref[...] * lk2_ref[...], axis=-1, keepdims=True)
    return jnp.exp(a) - jnp.exp(b) + lam_init


def _flash_body(qi_tab, ki_tab, q_ref, k_ref, v_ref, lq1_ref, lk1_ref, lq2_ref, lk2_ref, sub_ref,
                o_ref, m_ref, l_ref, acc_ref, *, scale, lam_init):
    t = pl.program_id(2)
    qi = qi_tab[t]
    ki = ki_tab[t]

    @pl.when(ki == 0)
    def _init():
        m_ref[...] = jnp.full_like(m_ref, -jnp.inf)
        l_ref[...] = jnp.zeros_like(l_ref)
        acc_ref[...] = jnp.zeros_like(acc_ref)

    def update(masked):
        vb = v_ref[...].astype(BF16)
        tq = q_ref.shape[0]
        tk = k_ref.shape[0]
        if masked:
            row = lax.broadcasted_iota(jnp.int32, (tq, tk), 0)
            col = lax.broadcasted_iota(jnp.int32, (tq, tk), 1)
            keep = col <= row
        for half in range(2):
            sl = slice(half * DA_DH, (half + 1) * DA_DH)
            s = _dot_nt(q_ref[:, sl], k_ref[:, sl].astype(BF16)) * scale
            if masked:
                s = jnp.where(keep, s, -jnp.inf)
            m_old = m_ref[half]
            m_new = jnp.maximum(m_old, jnp.max(s, axis=-1, keepdims=True))
            alpha = jnp.exp(m_old - m_new)
            p = jnp.exp(s - m_new)
            l_ref[half] = alpha * l_ref[half] + jnp.sum(p, axis=-1, keepdims=True)
            acc_ref[half] = alpha * acc_ref[half] + _dot(p.astype(BF16), vb)
            m_ref[half] = m_new

    @pl.when(ki < qi)
    def _off_diag():
        update(False)

    @pl.when(ki == qi)
    def _diag():
        update(True)
        lam = _diff_lambda(lq1_ref, lk1_ref, lq2_ref, lk2_ref, lam_init)
        o = acc_ref[0] / l_ref[0] - lam * (acc_ref[1] / l_ref[1])
        o_ref[...] = (_rms(o, SUBLN_EPS) * sub_ref[...] * (1.0 - lam_init)).astype(o_ref.dtype)


def _flash_prompt(q, k, v, lam_vecs, subln, batch, seq, lam_init, tq):
    nq = seq // tq
    qi_tab = np.concatenate([np.full(i + 1, i, np.int32) for i in range(nq)])
    ki_tab = np.concatenate([np.arange(i + 1, dtype=np.int32) for i in range(nq)])
    n_tri = int(qi_tab.shape[0])
    vec_spec = pl.BlockSpec((1, DA_DH), lambda b, h, t, qt, kt: (0, 0))
    grid_spec = pltpu.PrefetchScalarGridSpec(
        num_scalar_prefetch=2,
        grid=(batch, DA_HEADS, n_tri),
        in_specs=[
            pl.BlockSpec((tq, DA_VDIM), lambda b, h, t, qt, kt: (b * nq + qt[t], h)),
            pl.BlockSpec((tq, DA_VDIM), lambda b, h, t, qt, kt: (b * nq + kt[t], h)),
            pl.BlockSpec((tq, DA_VDIM), lambda b, h, t, qt, kt: (b * nq + kt[t], h)),
            vec_spec, vec_spec, vec_spec, vec_spec,
            pl.BlockSpec((1, DA_VDIM), lambda b, h, t, qt, kt: (0, 0)),
        ],
        out_specs=pl.BlockSpec((tq, DA_VDIM), lambda b, h, t, qt, kt: (b * nq + qt[t], h)),
        scratch_shapes=[
            pltpu.VMEM((2, tq, 1), F32),
            pltpu.VMEM((2, tq, 1), F32),
            pltpu.VMEM((2, tq, DA_VDIM), F32),
        ],
    )
    return pl.pallas_call(
        functools.partial(_flash_body, scale=DA_DH ** -0.5, lam_init=lam_init),
        grid_spec=grid_spec,
        out_shape=jax.ShapeDtypeStruct((batch * seq, DA_HEADS * DA_VDIM), BF16),
        compiler_params=_cparams(3),
        name="diff_attn_prompt",
    )(jnp.asarray(qi_tab), jnp.asarray(ki_tab), q, k, v, *lam_vecs, subln)


def _decode_body(pt_ref, q_ref, kn_ref, vn_ref, *rest, scale, lam_init, pages_per_step):
    k_refs = rest[:pages_per_step]
    v_refs = rest[pages_per_step:2 * pages_per_step]
    lq1_ref, lk1_ref, lq2_ref, lk2_ref, sub_ref, o_ref, m_ref, l_ref, acc_ref = rest[2 * pages_per_step:]
    step = pl.program_id(1)
    n_hh = 2 * DA_HEADS
    t_len = pages_per_step * k_refs[0].shape[0]

    @pl.when(step == 0)
    def _init():
        m_ref[...] = jnp.full_like(m_ref, -jnp.inf)
        l_ref[...] = jnp.zeros_like(l_ref)
        acc_ref[...] = jnp.zeros_like(acc_ref)

    qb = q_ref[...].astype(BF16)
    row_t = lax.broadcasted_iota(jnp.int32, (n_hh, t_len), 0)
    s = jnp.zeros((n_hh, t_len), F32)
    for hh in range(n_hh):
        k_hh = jnp.concatenate([kr[:, hh, :] for kr in k_refs], axis=0).astype(BF16)
        s = jnp.where(row_t == hh, _dot_nt(qb, k_hh), s)
    s = s * scale
    m_old = m_ref[...]
    m_new = jnp.maximum(m_old, jnp.max(s, axis=-1, keepdims=True))
    alpha = jnp.exp(m_old - m_new)
    p = jnp.exp(s - m_new)
    l_ref[...] = alpha * l_ref[...] + jnp.sum(p, axis=-1, keepdims=True)
    pb = p.astype(BF16)
    row_v = lax.broadcasted_iota(jnp.int32, (n_hh, DA_VDIM), 0)
    pv = jnp.zeros((n_hh, DA_VDIM), F32)
    for h in range(DA_HEADS):
        v_h = jnp.concatenate([vr[:, h, :] for vr in v_refs], axis=0).astype(BF16)
        pv = jnp.where((row_v >> 1) == h, _dot(pb, v_h), pv)
    acc_ref[...] = alpha * acc_ref[...] + pv
    m_ref[...] = m_new

    @pl.when(step == pl.num_programs(1) - 1)
    def _finish():
        s_new = jnp.sum(q_ref[...] * kn_ref[...], axis=-1, keepdims=True) * scale
        m_old2 = m_ref[...]
        m_fin = jnp.maximum(m_old2, s_new)
        alpha2 = jnp.exp(m_old2 - m_fin)
        p_new = jnp.exp(s_new - m_fin)
        l_fin = alpha2 * l_ref[...] + p_new
        a = (alpha2 * acc_ref[...] + p_new * vn_ref[...]) / l_fin
        lam = _diff_lambda(lq1_ref, lk1_ref, lq2_ref, lk2_ref, lam_init)
        for h in range(DA_HEADS):
            o = a[2 * h:2 * h + 1, :] - lam * a[2 * h + 1:2 * h + 2, :]
            o_ref[:, h * DA_VDIM:(h + 1) * DA_VDIM] = (
                _rms(o, SUBLN_EPS) * sub_ref[...] * (1.0 - lam_init)).astype(o_ref.dtype)


def _decode_attn(q3, kn3, vn3, cache_k, cache_v, page_table, lam_vecs, subln, layer, lam_init, pages_per_step):
    nb = q3.shape[0]
    n_hh = 2 * DA_HEADS
    page = cache_k.shape[2]
    n_pages = page_table.shape[1]
    n_steps = n_pages // pages_per_step

    def page_spec(r, heads, dim):
        return pl.BlockSpec((None, None, page, heads, dim),
                            lambda b, s, pt: (layer, pt[b, s * pages_per_step + r], 0, 0, 0))

    vec_spec = pl.BlockSpec((1, DA_DH), lambda b, s, pt: (0, 0))
    grid_spec = pltpu.PrefetchScalarGridSpec(
        num_scalar_prefetch=1,
        grid=(nb, n_steps),
        in_specs=[pl.BlockSpec((None, n_hh, DA_DH), lambda b, s, pt: (b, 0, 0)),
                  pl.BlockSpec((None, n_hh, DA_DH), lambda b, s, pt: (b, 0, 0)),
                  pl.BlockSpec((None, n_hh, DA_VDIM), lambda b, s, pt: (b, 0, 0))]
        + [page_spec(r, n_hh, DA_DH) for r in range(pages_per_step)]
        + [page_spec(r, DA_HEADS, DA_VDIM) for r in range(pages_per_step)]
        + [vec_spec, vec_spec, vec_spec, vec_spec,
           pl.BlockSpec((1, DA_VDIM), lambda b, s, pt: (0, 0))],
        out_specs=pl.BlockSpec((None, 1, DA_HEADS * DA_VDIM), lambda b, s, pt: (b, 0, 0)),
        scratch_shapes=[
            pltpu.VMEM((n_hh, 1), F32),
            pltpu.VMEM((n_hh, 1), F32),
            pltpu.VMEM((n_hh, DA_VDIM), F32),
        ],
    )
    return pl.pallas_call(
        functools.partial(_decode_body, scale=DA_DH ** -0.5, lam_init=lam_init, pages_per_step=pages_per_step),
        grid_spec=grid_spec,
        out_shape=jax.ShapeDtypeStruct((nb, 1, DA_HEADS * DA_VDIM), F32),
        compiler_params=_cparams(2),
        name="diff_attn_decode",
    )(page_table, q3, kn3, vn3, *([cache_k] * pages_per_step), *([cache_v] * pages_per_step), *lam_vecs, subln)


def _rope_tables(pos, d):
    inv = ROPE_THETA ** (-jnp.arange(0, d, 2, dtype=F32) / d)
    ang = pos.astype(F32)[:, None] * inv[None, :]
    return jnp.cos(ang), jnp.sin(ang)


def kernel(x_prompt, x_sample, state_ret, cache_k, cache_v, page_table, ffn_a_norm, ffn_a_w1, ffn_a_w3, ffn_a_w2, mix_norm, ffn_b_norm, ffn_b_w1, ffn_b_w3, ffn_b_w2, ev_w_in, ev_w_out, cm_ws, cm_bs, cm_ln_g, cm_ln_b, od_w_in, od_w_out, da_q_norm, da_k_norm, da_lam_q1, da_lam_k1, da_lam_q2, da_lam_k2, da_subln):
    batch, seq, d = x_prompt.shape
    nb, t_new, _ = x_sample.shape
    depth = ffn_a_norm.shape[0]
    assert t_new == 1 and seq % 1024 == 0 and nb == 8

    bm_p = 1024
    pos_p = jnp.arange(seq, dtype=jnp.int32)
    pos_s = PAST_LEN + jnp.arange(t_new, dtype=jnp.int32)
    cos_e, sin_e = _rope_tables(pos_p, R_DK)
    cos_es, sin_es = _rope_tables(pos_s, R_DK)
    cos_o, sin_o = _rope_tables(pos_p, DA_DH)
    cos_os, sin_os = _rope_tables(pos_s, DA_DH)
    c2_p = jnp.concatenate([cos_o, cos_o], axis=-1)
    s2_p = jnp.concatenate([-sin_o, sin_o], axis=-1)
    c2_s = jnp.broadcast_to(jnp.concatenate([cos_os, cos_os], axis=-1), (nb, DA_DH))
    s2_s = jnp.broadcast_to(jnp.concatenate([-sin_os, sin_os], axis=-1), (nb, DA_DH))
    tables = _retention_tables()

    as3 = lambda a: a.reshape(a.shape[0], 1, a.shape[1])
    ffn_a_norm3, ffn_b_norm3, mix_norm3 = as3(ffn_a_norm), as3(ffn_b_norm), as3(mix_norm)
    lng3, lnb3 = as3(cm_ln_g), as3(cm_ln_b)
    bs4 = cm_bs.reshape(cm_bs.shape + (1,))

    xp = x_prompt.reshape(batch * seq, d)
    xs = x_sample.reshape(nb * t_new, d)
    ret_p, ret_s, cmv_p, cmv_s = [], [], [], []
    kp_l, vp_l, ks_l, vs_l = [], [], [], []

    def ffn_pair(xp, xs, norm3, w1, w3, w2, l):
        xp = _ffn(xp, norm3, w1, w3, w2, l, bm=bm_p, bf=256, single_buffer_x=True)
        xs = _ffn(xs, norm3, w1, w3, w2, l, bm=nb, bf=512, single_buffer_x=False)
        return xp, xs

    for l in range(depth):
        xp, xs = ffn_pair(xp, xs, ffn_a_norm3, ffn_a_w1, ffn_a_w3, ffn_a_w2, l)
        if l % 2 == 0:
            e = l // 2
            even_in = ev_w_in.shape[-1]
            proj_p = _normproj(xp, mix_norm3, l, ev_w_in, e, 0, even_in, bm_p, 512, F32)
            mix_p, st_p, vn_p = _even_prompt(proj_p, cos_e, sin_e, tables, cm_ws, bs4, lng3, lnb3, e, batch, seq)
            proj_s = _normproj(xs, mix_norm3, l, ev_w_in, e, 0, even_in, nb, 512, F32)
            mix_s, st_s, vn_s = _even_sample(proj_s.reshape(nb, 1, even_in), cos_es, sin_es, tables,
                                             state_ret, cm_ws, bs4, lng3, lnb3, e)
            xp = _matres(mix_p, ev_w_out, xp, e, bm_p, 512)
            xs = _matres(mix_s.reshape(nb, -1), ev_w_out, xs, e, nb, 512)
            ret_p.append(st_p)
            ret_s.append(st_s)
            cmv_p.append(vn_p)
            cmv_s.append(vn_s)
        else:
            o = l // 2
            lam_init = 0.8 - 0.6 * math.exp(-0.3 * l)
            qn = da_q_norm[o].reshape(1, DA_DH)
            kn = da_k_norm[o].reshape(1, DA_DH)
            lam_vecs = [a[o].reshape(1, DA_DH) for a in (da_lam_q1, da_lam_k1, da_lam_q2, da_lam_k2)]
            subln = da_subln[o].reshape(1, DA_VDIM)
            tb = seq // bm_p
            q_p = _normproj(xp, mix_norm3, l, od_w_in, o, 0, d, bm_p, 512, BF16, qk=(qn, c2_p, s2_p, tb))
            k_p = _normproj(xp, mix_norm3, l, od_w_in, o, d, d, bm_p, 512, F32, qk=(kn, c2_p, s2_p, tb))
            v_p = _normproj(xp, mix_norm3, l, od_w_in, o, 2 * d, d, bm_p, 512, F32)
            att_p = _flash_prompt(q_p, k_p, v_p, lam_vecs, subln, batch, seq, lam_init, tq=512)
            q_s = _normproj(xs, mix_norm3, l, od_w_in, o, 0, d, nb, 512, F32, qk=(qn, c2_s, s2_s, 1))
            k_s = _normproj(xs, mix_norm3, l, od_w_in, o, d, d, nb, 512, F32, qk=(kn, c2_s, s2_s, 1))
            v_s = _normproj(xs, mix_norm3, l, od_w_in, o, 2 * d, d, nb, 512, F32)
            v_rep = jnp.repeat(v_s.reshape(nb, DA_HEADS, DA_VDIM), 2, axis=1)
            att_s = _decode_attn(q_s.reshape(nb, 2 * DA_HEADS, DA_DH), k_s.reshape(nb, 2 * DA_HEADS, DA_DH), v_rep,
                                 cache_k, cache_v, page_table, lam_vecs, subln, o, lam_init, pages_per_step=4)
            xp = _matres(att_p, od_w_out, xp, o, bm_p, 512)
            xs = _matres(att_s.reshape(nb, d), od_w_out, xs, o, nb, 512)
            kp_l.append(k_p.reshape(batch, seq, 2 * DA_HEADS, DA_DH))
            vp_l.append(v_p.reshape(batch, seq, DA_HEADS, DA_VDIM))
            ks_l.append(k_s.reshape(nb, t_new, 2 * DA_HEADS, DA_DH))
            vs_l.append(v_s.reshape(nb, t_new, DA_HEADS, DA_VDIM))
        xp, xs = ffn_pair(xp, xs, ffn_b_norm3, ffn_b_w1, ffn_b_w3, ffn_b_w2, l)

    return (xp.reshape(batch, seq, d), xs.reshape(nb, t_new, d),
            jnp.stack(ret_p), jnp.stack(ret_s), jnp.stack(cmv_p), jnp.stack(cmv_s),
            jnp.stack(kp_l), jnp.stack(vp_l), jnp.stack(ks_l), jnp.stack(vs_l))
```

```python
import functools
import math

import numpy as np
import jax
import jax.numpy as jnp
from jax import lax
from jax.experimental import pallas as pl
from jax.experimental.pallas import tpu as pltpu

F32 = jnp.float32
BF16 = jnp.bfloat16

PAST_LEN = 16384
R_HEADS = 4
R_DK = 256
R_DV = 256
R_CHUNK = 128
CM_GROUPS = 4
CM_GDIM = 256
DA_HEADS = 8
DA_DH = 128
DA_VDIM = 256
LOG2_DA_DH = 7
LOG2_DA_VDIM = 8
ROPE_THETA = 10000.0
EPS = 1e-6
SUBLN_EPS = 1e-5
LN_EPS = 1e-5

VMEM_LIMIT_BYTES = 56 * 1024 * 1024


def _cparams(n_axes):
    return pltpu.CompilerParams(
        dimension_semantics=("arbitrary",) * n_axes,
        vmem_limit_bytes=VMEM_LIMIT_BYTES,
    )


def _silu(a):
    return a / (1.0 + jnp.exp(-a))


def _gelu_tanh(x):
    c = math.sqrt(2.0 / math.pi)
    return x * (0.5 * (1.0 + jnp.tanh(c * (x + 0.044715 * (x * x * x)))))


def _rms(x, eps):
    return x * lax.rsqrt(jnp.mean(x * x, axis=-1, keepdims=True) + eps)


def _dot(a, b):
    return jnp.dot(a, b, preferred_element_type=F32)


def _dot_nt(a, b):
    return lax.dot_general(a, b, (((1,), (1,)), ((), ())), preferred_element_type=F32)


def _ffn_body(x_ref, xs_ref, g_ref, w1_ref, w3_ref, w2_ref, o_ref, os_ref, xn_ref, xsn_ref, *, n_split):
    i = pl.program_id(0)
    j = pl.program_id(1)

    @pl.when(j == 0)
    def _init():
        x = x_ref[...]
        xn_ref[...] = (_rms(x, EPS) * g_ref[...]).astype(BF16)
        o_ref[...] = x

    @pl.when((i == 0) & (j == 0))
    def _init_sample():
        xs = xs_ref[...]
        xsn_ref[...] = (_rms(xs, EPS) * g_ref[...]).astype(BF16)
        os_ref[...] = xs

    w1 = w1_ref[...].astype(BF16)
    w3 = w3_ref[...].astype(BF16)

    def hidden(xn):
        a = _dot(xn, w1)
        return ((_silu(a) * _dot(xn, w3)) * 0.5).astype(BF16)

    h = hidden(xn_ref[...])
    d = o_ref.shape[-1]
    w = d // n_split
    for n in range(n_split):
        sl = slice(n * w, (n + 1) * w)
        o_ref[:, sl] += _dot(h, w2_ref[:, sl].astype(BF16))

    @pl.when(i == 0)
    def _sample():
        os_ref[...] += _dot(hidden(xsn_ref[...]), w2_ref[...].astype(BF16))


def _ffn(x, xs, norm3, w1, w3, w2, layer, bm, bf):
    m, d = x.shape
    ms = xs.shape[0]
    ff = w1.shape[-1]
    return pl.pallas_call(
        functools.partial(_ffn_body, n_split=4),
        grid=(m // bm, ff // bf),
        in_specs=[
            pl.BlockSpec((bm, d), lambda i, j: (i, 0), pipeline_mode=pl.Buffered(1)),
            pl.BlockSpec((ms, d), lambda i, j: (0, 0)),
            pl.BlockSpec((None, 1, d), lambda i, j: (layer, 0, 0)),
            pl.BlockSpec((None, d, bf), lambda i, j: (layer, 0, j)),
            pl.BlockSpec((None, d, bf), lambda i, j: (layer, 0, j)),
            pl.BlockSpec((None, bf, d), lambda i, j: (layer, j, 0)),
        ],
        out_specs=[
            pl.BlockSpec((bm, d), lambda i, j: (i, 0)),
            pl.BlockSpec((ms, d), lambda i, j: (0, 0)),
        ],
        out_shape=[jax.ShapeDtypeStruct((m, d), F32), jax.ShapeDtypeStruct((ms, d), F32)],
        scratch_shapes=[pltpu.VMEM((bm, d), BF16), pltpu.VMEM((ms, d), BF16)],
        compiler_params=_cparams(2),
        name="ffn",
    )(x, xs, norm3, w1, w3, w2)


def _normproj_body(x_ref, g_ref, w_ref, o_ref, xn_ref):
    @pl.when(pl.program_id(1) == 0)
    def _init():
        xn_ref[...] = (_rms(x_ref[...], EPS) * g_ref[...]).astype(BF16)

    o_ref[...] = _dot(xn_ref[...], w_ref[...].astype(BF16)).astype(o_ref.dtype)


def _normproj(x, norm3, norm_layer, w, layer, bm, bn):
    m, d = x.shape
    ncols = w.shape[-1]
    return pl.pallas_call(
        _normproj_body,
        grid=(m // bm, ncols // bn),
        in_specs=[
            pl.BlockSpec((bm, d), lambda i, j: (i, 0)),
            pl.BlockSpec((None, 1, d), lambda i, j: (norm_layer, 0, 0)),
            pl.BlockSpec((None, d, bn), lambda i, j: (layer, 0, j)),
        ],
        out_specs=pl.BlockSpec((bm, bn), lambda i, j: (i, j)),
        out_shape=jax.ShapeDtypeStruct((m, ncols), F32),
        scratch_shapes=[pltpu.VMEM((bm, d), BF16)],
        compiler_params=_cparams(2),
        name="normproj",
    )(x, norm3, w)


def _normproj_qkv_body(x_ref, g_ref, w_ref, qn_ref, kn_ref, c_ref, s_ref, q_ref, k_ref, v_ref, xn_ref, *, nj):
    j = pl.program_id(1)

    @pl.when(j == 0)
    def _init():
        xn_ref[...] = (_rms(x_ref[...], EPS) * g_ref[...]).astype(BF16)

    y = _dot(xn_ref[...], w_ref[...].astype(BF16))

    def norm_rope(hn_ref, o_ref):
        hn = hn_ref[...]
        c = c_ref[...]
        s = s_ref[...]
        for t in range(y.shape[-1] // DA_DH):
            sl = slice(t * DA_DH, (t + 1) * DA_DH)
            yn = _rms(y[:, sl], EPS) * hn
            o_ref[:, sl] = (yn * c + pltpu.roll(yn, DA_DH // 2, axis=1) * s).astype(o_ref.dtype)

    @pl.when(j < nj)
    def _q():
        norm_rope(qn_ref, q_ref)

    @pl.when((j >= nj) & (j < 2 * nj))
    def _k():
        norm_rope(kn_ref, k_ref)

    @pl.when(j >= 2 * nj)
    def _v():
        v_ref[...] = y


def _normproj_qkv(x, norm3, norm_layer, w, layer, bm, bn, q_dtype, qn, kn, c2, s2, table_blocks):
    m, d = x.shape
    nj = d // bn
    clip = lambda j, lo: jnp.clip(j - lo, 0, nj - 1)
    return pl.pallas_call(
        functools.partial(_normproj_qkv_body, nj=nj),
        grid=(m // bm, 3 * nj),
        in_specs=[
            pl.BlockSpec((bm, d), lambda i, j: (i, 0)),
            pl.BlockSpec((None, 1, d), lambda i, j: (norm_layer, 0, 0)),
            pl.BlockSpec((None, d, bn), lambda i, j: (layer, 0, j)),
            pl.BlockSpec((1, DA_DH), lambda i, j: (0, 0)),
            pl.BlockSpec((1, DA_DH), lambda i, j: (0, 0)),
            pl.BlockSpec((bm, DA_DH), lambda i, j: (i % table_blocks, 0)),
            pl.BlockSpec((bm, DA_DH), lambda i, j: (i % table_blocks, 0)),
        ],
        out_specs=[
            pl.BlockSpec((bm, bn), lambda i, j: (i, clip(j, 0))),
            pl.BlockSpec((bm, bn), lambda i, j: (i, clip(j, nj))),
            pl.BlockSpec((bm, bn), lambda i, j: (i, clip(j, 2 * nj))),
        ],
        out_shape=[
            jax.ShapeDtypeStruct((m, d), q_dtype),
            jax.ShapeDtypeStruct((m, d), F32),
            jax.ShapeDtypeStruct((m, d), F32),
        ],
        scratch_shapes=[pltpu.VMEM((bm, d), BF16)],
        compiler_params=_cparams(2),
        name="normproj_qkv",
    )(x, norm3, w, qn, kn, c2, s2)


def _matres_body(a_ref, w_ref, r_ref, o_ref):
    o_ref[...] = r_ref[...] + _dot(a_ref[...].astype(BF16), w_ref[...].astype(BF16))


def _matres(a, w, res, layer, bm, bn):
    m, k = a.shape
    n = w.shape[-1]
    return pl.pallas_call(
        _matres_body,
        grid=(m // bm, n // bn),
        in_specs=[
            pl.BlockSpec((bm, k), lambda i, j: (i, 0)),
            pl.BlockSpec((None, k, bn), lambda i, j: (layer, 0, j)),
            pl.BlockSpec((bm, bn), lambda i, j: (i, j)),
        ],
        out_specs=pl.BlockSpec((bm, bn), lambda i, j: (i, j)),
        out_shape=jax.ShapeDtypeStruct((m, n), F32),
        compiler_params=_cparams(2),
        name="matres",
    )(a, w, res)


def _retention_tables():
    expo = -5.0 - 7.0 * np.arange(R_HEADS, dtype=np.float64) / max(R_HEADS - 1, 1)
    log_g = np.log1p(-np.exp2(expo))
    idx = np.arange(R_CHUNK, dtype=np.float64)
    diff = idx[:, None] - idx[None, :]
    decay = np.where(diff >= 0, np.exp(log_g[:, None, None] * np.maximum(diff, 0.0)), 0.0)
    row_decay = np.exp(log_g[:, None] * (idx + 1.0))[:, :, None]
    k_decay = np.exp(log_g[:, None] * (R_CHUNK - 1.0 - idx))[:, :, None]
    state_decay = tuple(float(v) for v in np.exp(log_g * R_CHUNK))
    gamma = tuple(float(v) for v in np.exp(log_g))
    return (decay.astype(np.float32), row_decay.astype(np.float32), k_decay.astype(np.float32),
            state_decay, gamma)


def _rope_half(x, cos, sin):
    half = x.shape[-1] // 2
    x1 = x[:, :half]
    x2 = x[:, half:]
    return jnp.concatenate([x1 * cos - x2 * sin, x2 * cos + x1 * sin], axis=-1)


def _layer_norm(x, g, b):
    xc = x - jnp.mean(x, axis=-1, keepdims=True)
    return xc * lax.rsqrt(jnp.mean(xc * xc, axis=-1, keepdims=True) + LN_EPS) * g + b


def _even_prompt_body(p_ref, cos_ref, sin_ref, dec_ref, rd_ref, kd_ref, ws_ref, bs_ref, lng_ref, lnb_ref,
                      o_ref, st_ref, vn_ref, state, *, state_decay):
    c = pl.program_id(1)
    last = pl.num_programs(1) - 1

    @pl.when(c == 0)
    def _zero():
        state[...] = jnp.zeros_like(state)

    cos = cos_ref[...]
    sin = sin_ref[...]
    qk_w = R_HEADS * R_DK
    for h in range(R_HEADS):
        q = _rope_half(p_ref[:, h * R_DK:(h + 1) * R_DK], cos, sin)
        k = _rope_half(p_ref[:, qk_w + h * R_DK:qk_w + (h + 1) * R_DK], cos, sin) * (R_DK ** -0.5)
        v = p_ref[:, 2 * qk_w + h * R_DV:2 * qk_w + (h + 1) * R_DV].astype(BF16)
        g = p_ref[:, 3 * qk_w + h * R_DV:3 * qk_w + (h + 1) * R_DV]
        qb = q.astype(BF16)
        s = _dot_nt(qb, k.astype(BF16)) * dec_ref[h]
        st = state[h]
        o = _dot(s.astype(BF16), v) + _dot(qb, st.astype(BF16)) * rd_ref[h]
        kd_t = (k * kd_ref[h]).T.astype(BF16)
        state[h] = st * state_decay[h] + _dot(kd_t, v)
        o_ref[:, h * R_DV:(h + 1) * R_DV] = (_silu(g) * _rms(o, EPS)).astype(o_ref.dtype)

    u0 = 3 * qk_w + R_HEADS * R_DV
    cmw = CM_GROUPS * CM_GDIM
    vn = _layer_norm(_gelu_tanh(p_ref[:, u0 + cmw:u0 + 2 * cmw]), lng_ref[...], lnb_ref[...])
    row = lax.broadcasted_iota(jnp.int32, (R_CHUNK, R_CHUNK), 0)
    col = lax.broadcasted_iota(jnp.int32, (R_CHUNK, R_CHUNK), 1)
    for gi in range(CM_GROUPS):
        sl = slice(gi * CM_GDIM, (gi + 1) * CM_GDIM)
        w = jnp.where(row >= col, ws_ref[gi], 0.0).astype(BF16)
        mixed = _dot(w, vn[:, sl].astype(BF16)) + bs_ref[gi]
        u = p_ref[:, u0 + gi * CM_GDIM:u0 + (gi + 1) * CM_GDIM]
        o_ref[:, R_HEADS * R_DV + gi * CM_GDIM:R_HEADS * R_DV + (gi + 1) * CM_GDIM] = (
            _gelu_tanh(u) * mixed).astype(o_ref.dtype)

    @pl.when(c == last)
    def _emit():
        st_ref[...] = state[...]
        vn_ref[...] = vn


def _even_prompt(proj, cos, sin, tables, ws, bs4, lng3, lnb3, e, batch, seq):
    decay, row_decay, k_decay, state_decay, _ = tables
    nc = seq // R_CHUNK
    width = proj.shape[-1]
    cmw = CM_GROUPS * CM_GDIM
    out_w = R_HEADS * R_DV + cmw
    const3 = lambda b, c: (0, 0, 0)
    return pl.pallas_call(
        functools.partial(_even_prompt_body, state_decay=state_decay),
        grid=(batch, nc),
        in_specs=[
            pl.BlockSpec((R_CHUNK, width), lambda b, c: (b * nc + c, 0)),
            pl.BlockSpec((R_CHUNK, R_DK // 2), lambda b, c: (c, 0)),
            pl.BlockSpec((R_CHUNK, R_DK // 2), lambda b, c: (c, 0)),
            pl.BlockSpec(decay.shape, const3),
            pl.BlockSpec(row_decay.shape, const3),
            pl.BlockSpec(k_decay.shape, const3),
            pl.BlockSpec((None, CM_GROUPS, R_CHUNK, R_CHUNK), lambda b, c: (e, 0, 0, 0)),
            pl.BlockSpec((None, CM_GROUPS, R_CHUNK, 1), lambda b, c: (e, 0, 0, 0)),
            pl.BlockSpec((None, 1, cmw), lambda b, c: (e, 0, 0)),
            pl.BlockSpec((None, 1, cmw), lambda b, c: (e, 0, 0)),
        ],
        out_specs=[
            pl.BlockSpec((R_CHUNK, out_w), lambda b, c: (b * nc + c, 0)),
            pl.BlockSpec((None, R_HEADS, R_DK, R_DV), lambda b, c: (b, 0, 0, 0)),
            pl.BlockSpec((None, R_CHUNK, cmw), lambda b, c: (b, 0, 0)),
        ],
        out_shape=[
            jax.ShapeDtypeStruct((batch * seq, out_w), BF16),
            jax.ShapeDtypeStruct((batch, R_HEADS, R_DK, R_DV), F32),
            jax.ShapeDtypeStruct((batch, R_CHUNK, cmw), F32),
        ],
        scratch_shapes=[pltpu.VMEM((R_HEADS, R_DK, R_DV), F32)],
        compiler_params=_cparams(2),
        name="even_prompt",
    )(proj, cos, sin, jnp.asarray(decay), jnp.asarray(row_decay), jnp.asarray(k_decay), ws, bs4, lng3, lnb3)


def _even_sample_body(p_ref, cos_ref, sin_ref, st_in_ref, ws_ref, bs_ref, lng_ref, lnb_ref,
                      o_ref, st_ref, vn_ref, *, gamma):
    cos = cos_ref[...]
    sin = sin_ref[...]
    qk_w = R_HEADS * R_DK
    r0 = lax.broadcasted_iota(jnp.int32, (R_DK, R_DK), 0)
    r1 = lax.broadcasted_iota(jnp.int32, (R_DK, R_DK), 1)
    eye = jnp.where(r0 == r1, 1.0, 0.0).astype(BF16)
    for h in range(R_HEADS):
        q = _rope_half(p_ref[:, h * R_DK:(h + 1) * R_DK], cos, sin)
        k = _rope_half(p_ref[:, qk_w + h * R_DK:qk_w + (h + 1) * R_DK], cos, sin) * (R_DK ** -0.5)
        v = p_ref[:, 2 * qk_w + h * R_DV:2 * qk_w + (h + 1) * R_DV]
        g = p_ref[:, 3 * qk_w + h * R_DV:3 * qk_w + (h + 1) * R_DV]
        st = st_in_ref[h]
        s = jnp.sum(q * k, axis=-1, keepdims=True)
        qb = jnp.broadcast_to(q, (8, R_DK)).astype(BF16)
        cross = _dot(qb, st.astype(BF16))[0:1, :]
        o = s * v + cross * gamma[h]
        kb = jnp.broadcast_to(k, (128, R_DK)).astype(BF16)
        kcol = _dot_nt(eye, kb)
        kcol = jnp.concatenate([kcol] * (R_DV // 128), axis=-1)
        vb = v.astype(BF16).astype(F32)
        st_ref[h] = st * gamma[h] + kcol * vb
        o_ref[:, h * R_DV:(h + 1) * R_DV] = (_silu(g) * _rms(o, EPS)).astype(o_ref.dtype)

    u0 = 3 * qk_w + R_HEADS * R_DV
    cmw = CM_GROUPS * CM_GDIM
    vn = _layer_norm(_gelu_tanh(p_ref[:, u0 + cmw:u0 + 2 * cmw]), lng_ref[...], lnb_ref[...])
    vn_ref[...] = vn
    for gi in range(CM_GROUPS):
        sl = slice(gi * CM_GDIM, (gi + 1) * CM_GDIM)
        mixed = ws_ref[gi, 0:1, 0:1] * vn[:, sl] + bs_ref[gi, 0:1, :]
        u = p_ref[:, u0 + gi * CM_GDIM:u0 + (gi + 1) * CM_GDIM]
        o_ref[:, R_HEADS * R_DV + gi * CM_GDIM:R_HEADS * R_DV + (gi + 1) * CM_GDIM] = (
            _gelu_tanh(u) * mixed).astype(o_ref.dtype)


def _even_sample(proj3, cos, sin, tables, st_in, ws, bs4, lng3, lnb3, e):
    gamma = tables[4]
    nb = proj3.shape[0]
    width = proj3.shape[-1]
    cmw = CM_GROUPS * CM_GDIM
    out_w = R_HEADS * R_DV + cmw
    return pl.pallas_call(
        functools.partial(_even_sample_body, gamma=gamma),
        grid=(nb,),
        in_specs=[
            pl.BlockSpec((None, 1, width), lambda b: (b, 0, 0)),
            pl.BlockSpec((1, R_DK // 2), lambda b: (0, 0)),
            pl.BlockSpec((1, R_DK // 2), lambda b: (0, 0)),
            pl.BlockSpec((None, None, R_HEADS, R_DK, R_DV), lambda b: (e, b, 0, 0, 0)),
            pl.BlockSpec((None, CM_GROUPS, R_CHUNK, R_CHUNK), lambda b: (e, 0, 0, 0)),
            pl.BlockSpec((None, CM_GROUPS, R_CHUNK, 1), lambda b: (e, 0, 0, 0)),
            pl.BlockSpec((None, 1, cmw), lambda b: (e, 0, 0)),
            pl.BlockSpec((None, 1, cmw), lambda b: (e, 0, 0)),
        ],
        out_specs=[
            pl.BlockSpec((None, 1, out_w), lambda b: (b, 0, 0)),
            pl.BlockSpec((None, R_HEADS, R_DK, R_DV), lambda b: (b, 0, 0, 0)),
            pl.BlockSpec((None, 1, cmw), lambda b: (b, 0, 0)),
        ],
        out_shape=[
            jax.ShapeDtypeStruct((nb, 1, out_w), F32),
            jax.ShapeDtypeStruct((nb, R_HEADS, R_DK, R_DV), F32),
            jax.ShapeDtypeStruct((nb, 1, cmw), F32),
        ],
        compiler_params=_cparams(1),
        name="even_sample",
    )(proj3, cos, sin, st_in, ws, bs4, lng3, lnb3)


def _diff_lambda(lq1_ref, lk1_ref, lq2_ref, lk2_ref, lam_init):
    a = jnp.sum(lq1_ref[...] * lk1_ref[...], axis=-1, keepdims=True)
    b = jnp.sum(lq2_ref[...] * lk2_ref[...], axis=-1, keepdims=True)
    return jnp.exp(a) - jnp.exp(b) + lam_init


def _flash_body(qi_tab, ki_tab, q_ref, k_ref, v_ref, lq1_ref, lk1_ref, lq2_ref, lk2_ref, sub_ref,
                o_ref, m_ref, l_ref, acc_ref, *, scale, lam_init):
    t = pl.program_id(2)
    qi = qi_tab[t]
    ki = ki_tab[t]
    tq = q_ref.shape[0]
    tk = k_ref.shape[0]
    r = tq // tk
    c = scale * math.log2(math.e)

    @pl.when(ki == 0)
    def _init():
        m_ref[...] = jnp.full_like(m_ref, -jnp.inf)
        l_ref[...] = jnp.zeros_like(l_ref)
        acc_ref[...] = jnp.zeros_like(acc_ref)

    def update(modes):
        vb = v_ref[...].astype(BF16)
        if "tri" in modes:
            row = lax.broadcasted_iota(jnp.int32, (tk, tk), 0)
            col = lax.broadcasted_iota(jnp.int32, (tk, tk), 1)
            keep = col <= row
        for half in range(2):
            sl = slice(half * DA_DH, (half + 1) * DA_DH)
            kb = k_ref[:, sl].astype(BF16)
            for rb, mode in enumerate(modes):
                if mode == "skip":
                    continue
                rows = slice(rb * tk, (rb + 1) * tk)
                s = _dot_nt(q_ref[rows, sl], kb)
                if mode == "tri":
                    s = jnp.where(keep, s, -jnp.inf)
                m_old = m_ref[half, rows]
                m_new = jnp.maximum(m_old, jnp.max(s, axis=-1, keepdims=True))
                alpha = jnp.exp2(c * (m_old - m_new))
                p = jnp.exp2(c * s - c * m_new)
                l_ref[half, rows] = alpha * l_ref[half, rows] + jnp.sum(p, axis=-1, keepdims=True)
                acc_ref[half, rows] = alpha * acc_ref[half, rows] + _dot(p.astype(BF16), vb)
                m_ref[half, rows] = m_new

    @pl.when(ki < qi * r)
    def _below_diag():
        update(["full"] * r)

    for kd in range(r):
        @pl.when(ki == qi * r + kd)
        def _on_diag(kd=kd):
            update(["skip"] * kd + ["tri"] + ["full"] * (r - 1 - kd))

    @pl.when(ki == qi * r + (r - 1))
    def _finish():
        lam = _diff_lambda(lq1_ref, lk1_ref, lq2_ref, lk2_ref, lam_init)
        o = acc_ref[0] / l_ref[0] - lam * (acc_ref[1] / l_ref[1])
        o_ref[...] = (_rms(o, SUBLN_EPS) * sub_ref[...] * (1.0 - lam_init)).astype(o_ref.dtype)


def _flash_prompt(q, k, v, lam_vecs, subln, batch, seq, lam_init, tq, tk):
    nq = seq // tq
    r = tq // tk
    qi_tab = np.concatenate([np.full((i + 1) * r, i, np.int32) for i in range(nq)])
    ki_tab = np.concatenate([np.arange((i + 1) * r, dtype=np.int32) for i in range(nq)])
    n_tri = int(qi_tab.shape[0])
    vec_spec = pl.BlockSpec((1, DA_DH), lambda b, h, t, qt, kt: (0, 0))
    grid_spec = pltpu.PrefetchScalarGridSpec(
        num_scalar_prefetch=2,
        grid=(batch, DA_HEADS, n_tri),
        in_specs=[
            pl.BlockSpec((tq, DA_VDIM), lambda b, h, t, qt, kt: (b * nq + qt[t], h)),
            pl.BlockSpec((tk, DA_VDIM), lambda b, h, t, qt, kt: (b * nq * r + kt[t], h)),
            pl.BlockSpec((tk, DA_VDIM), lambda b, h, t, qt, kt: (b * nq * r + kt[t], h)),
            vec_spec, vec_spec, vec_spec, vec_spec,
            pl.BlockSpec((1, DA_VDIM), lambda b, h, t, qt, kt: (0, 0)),
        ],
        out_specs=pl.BlockSpec((tq, DA_VDIM), lambda b, h, t, qt, kt: (b * nq + qt[t], h)),
        scratch_shapes=[
            pltpu.VMEM((2, tq, 1), F32),
            pltpu.VMEM((2, tq, 1), F32),
            pltpu.VMEM((2, tq, DA_VDIM), F32),
        ],
    )
    return pl.pallas_call(
        functools.partial(_flash_body, scale=DA_DH ** -0.5, lam_init=lam_init),
        grid_spec=grid_spec,
        out_shape=jax.ShapeDtypeStruct((batch * seq, DA_HEADS * DA_VDIM), BF16),
        compiler_params=_cparams(3),
        name="diff_attn_prompt",
    )(jnp.asarray(qi_tab), jnp.asarray(ki_tab), q, k, v, *lam_vecs, subln)


def _decode_body(pt_ref, q_ref, kn_ref, vn_ref, *rest, scale, lam_init, pages_per_step):
    k_refs = rest[:pages_per_step]
    v_refs = rest[pages_per_step:2 * pages_per_step]
    (exp_ref, own_ref, lq1_ref, lk1_ref, lq2_ref, lk2_ref, sub_ref,
     o_ref, m_ref, l_ref, acc_ref) = rest[2 * pages_per_step:]
    step = pl.program_id(1)
    n_hh = 2 * DA_HEADS
    page = k_refs[0].shape[0] // n_hh
    t_len = pages_per_step * page

    @pl.when(step == 0)
    def _init():
        m_ref[...] = jnp.full_like(m_ref, -jnp.inf)
        l_ref[...] = jnp.zeros_like(l_ref)
        acc_ref[...] = jnp.zeros_like(acc_ref)

    qb = q_ref[...].astype(BF16)
    row_t = lax.broadcasted_iota(jnp.int32, (n_hh, t_len), 0)
    s = jnp.zeros((n_hh, t_len), F32)
    for hh in range(n_hh):
        k_hh = jnp.concatenate([kr[pl.ds(hh, page, stride=n_hh), :] for kr in k_refs], axis=0).astype(BF16)
        s = jnp.where(row_t == hh, _dot_nt(qb, k_hh), s)
    s = s * scale
    m_old = m_ref[...]
    m_new = jnp.maximum(m_old, jnp.max(s, axis=-1, keepdims=True))
    alpha = jnp.exp(m_old - m_new)
    p = jnp.exp(s - m_new)
    l_ref[...] = alpha * l_ref[...] + jnp.sum(p, axis=-1, keepdims=True)
    pb = p.astype(BF16)
    pv = jnp.zeros((n_hh, DA_VDIM), F32)
    for r, vr in enumerate(v_refs):
        spread = _dot(pb[:, r * page:(r + 1) * page], exp_ref[...]) * own_ref[...]
        pv += _dot(spread.astype(BF16), vr[...].astype(BF16))
    acc_ref[...] = alpha * acc_ref[...] + pv
    m_ref[...] = m_new

    @pl.when(step == pl.num_programs(1) - 1)
    def _finish():
        s_new = jnp.sum(q_ref[...] * kn_ref[...], axis=-1, keepdims=True) * scale
        m_old2 = m_ref[...]
        m_fin = jnp.maximum(m_old2, s_new)
        alpha2 = jnp.exp(m_old2 - m_fin)
        p_new = jnp.exp(s_new - m_fin)
        l_fin = alpha2 * l_ref[...] + p_new
        a = (alpha2 * acc_ref[...] + p_new * vn_ref[...]) / l_fin
        lam = _diff_lambda(lq1_ref, lk1_ref, lq2_ref, lk2_ref, lam_init)
        for h in range(DA_HEADS):
            o = a[2 * h:2 * h + 1, :] - lam * a[2 * h + 1:2 * h + 2, :]
            o_ref[:, h * DA_VDIM:(h + 1) * DA_VDIM] = (
                _rms(o, SUBLN_EPS) * sub_ref[...] * (1.0 - lam_init)).astype(o_ref.dtype)


def _decode_attn(q3, kn3, vn3, cache_k, cache_v, page_table, lam_vecs, subln, layer, lam_init, pages_per_step):
    nb = q3.shape[0]
    n_hh = 2 * DA_HEADS
    n_layers, n_pool, page = cache_k.shape[:3]
    n_pages = page_table.shape[1]
    n_steps = n_pages // pages_per_step
    cache_k = cache_k.reshape(n_layers, n_pool, page * n_hh, DA_DH)
    cache_v = cache_v.reshape(n_layers, n_pool, page * DA_HEADS, DA_VDIM)

    def page_spec(r, heads, dim):
        return pl.BlockSpec((None, None, page * heads, dim),
                            lambda b, s, pt: (layer, pt[b, s * pages_per_step + r], 0, 0))

    rows_v = np.arange(page * DA_HEADS)
    spread = (rows_v[None, :] // DA_HEADS == np.arange(page)[:, None]).astype(np.float32)
    own = (rows_v[None, :] % DA_HEADS == np.arange(n_hh)[:, None] // 2).astype(np.float32)

    vec_spec = pl.BlockSpec((1, DA_DH), lambda b, s, pt: (0, 0))
    grid_spec = pltpu.PrefetchScalarGridSpec(
        num_scalar_prefetch=1,
        grid=(nb, n_steps),
        in_specs=[pl.BlockSpec((None, n_hh, DA_DH), lambda b, s, pt: (b, 0, 0)),
                  pl.BlockSpec((None, n_hh, DA_DH), lambda b, s, pt: (b, 0, 0)),
                  pl.BlockSpec((None, n_hh, DA_VDIM), lambda b, s, pt: (b, 0, 0))]
        + [page_spec(r, n_hh, DA_DH) for r in range(pages_per_step)]
        + [page_spec(r, DA_HEADS, DA_VDIM) for r in range(pages_per_step)]
        + [pl.BlockSpec(spread.shape, lambda b, s, pt: (0, 0)),
           pl.BlockSpec(own.shape, lambda b, s, pt: (0, 0))]
        + [vec_spec, vec_spec, vec_spec, vec_spec,
           pl.BlockSpec((1, DA_VDIM), lambda b, s, pt: (0, 0))],
        out_specs=pl.BlockSpec((None, 1, DA_HEADS * DA_VDIM), lambda b, s, pt: (b, 0, 0)),
        scratch_shapes=[
            pltpu.VMEM((n_hh, 1), F32),
            pltpu.VMEM((n_hh, 1), F32),
            pltpu.VMEM((n_hh, DA_VDIM), F32),
        ],
    )
    return pl.pallas_call(
        functools.partial(_decode_body, scale=DA_DH ** -0.5, lam_init=lam_init, pages_per_step=pages_per_step),
        grid_spec=grid_spec,
        out_shape=jax.ShapeDtypeStruct((nb, 1, DA_HEADS * DA_VDIM), F32),
        compiler_params=_cparams(2),
        name="diff_attn_decode",
    )(page_table, q3, kn3, vn3, *([cache_k] * pages_per_step), *([cache_v] * pages_per_step),
      jnp.asarray(spread, BF16), jnp.asarray(own), *lam_vecs, subln)


def _rope_tables(pos, d):
    inv = ROPE_THETA ** (-jnp.arange(0, d, 2, dtype=F32) / d)
    ang = pos.astype(F32)[:, None] * inv[None, :]
    return jnp.cos(ang), jnp.sin(ang)


def kernel(x_prompt, x_sample, state_ret, cache_k, cache_v, page_table, ffn_a_norm, ffn_a_w1, ffn_a_w3, ffn_a_w2, mix_norm, ffn_b_norm, ffn_b_w1, ffn_b_w3, ffn_b_w2, ev_w_in, ev_w_out, cm_ws, cm_bs, cm_ln_g, cm_ln_b, od_w_in, od_w_out, da_q_norm, da_k_norm, da_lam_q1, da_lam_k1, da_lam_q2, da_lam_k2, da_subln):
    batch, seq, d = x_prompt.shape
    nb, t_new, _ = x_sample.shape
    depth = ffn_a_norm.shape[0]
    assert t_new == 1 and seq % 1024 == 0 and nb == 8

    bm_p = 1024
    pos_p = jnp.arange(seq, dtype=jnp.int32)
    pos_s = PAST_LEN + jnp.arange(t_new, dtype=jnp.int32)
    cos_e, sin_e = _rope_tables(pos_p, R_DK)
    cos_es, sin_es = _rope_tables(pos_s, R_DK)
    cos_o, sin_o = _rope_tables(pos_p, DA_DH)
    cos_os, sin_os = _rope_tables(pos_s, DA_DH)
    c2_p = jnp.concatenate([cos_o, cos_o], axis=-1)
    s2_p = jnp.concatenate([-sin_o, sin_o], axis=-1)
    c2_s = jnp.broadcast_to(jnp.concatenate([cos_os, cos_os], axis=-1), (nb, DA_DH))
    s2_s = jnp.broadcast_to(jnp.concatenate([-sin_os, sin_os], axis=-1), (nb, DA_DH))
    tables = _retention_tables()

    as3 = lambda a: a.reshape(a.shape[0], 1, a.shape[1])
    ffn_a_norm3, ffn_b_norm3, mix_norm3 = as3(ffn_a_norm), as3(ffn_b_norm), as3(mix_norm)
    lng3, lnb3 = as3(cm_ln_g), as3(cm_ln_b)
    bs4 = cm_bs.reshape(cm_bs.shape + (1,))

    xp = x_prompt.reshape(batch * seq, d)
    xs = x_sample.reshape(nb * t_new, d)
    ret_p, ret_s, cmv_p, cmv_s = [], [], [], []
    kp_l, vp_l, ks_l, vs_l = [], [], [], []

    def ffn_pair(xp, xs, norm3, w1, w3, w2, l):
        return _ffn(xp, xs, norm3, w1, w3, w2, l, bm=bm_p, bf=256)

    for l in range(depth):
        xp, xs = ffn_pair(xp, xs, ffn_a_norm3, ffn_a_w1, ffn_a_w3, ffn_a_w2, l)
        if l % 2 == 0:
            e = l // 2
            even_in = ev_w_in.shape[-1]
            proj_p = _normproj(xp, mix_norm3, l, ev_w_in, e, bm_p, 512)
            mix_p, st_p, vn_p = _even_prompt(proj_p, cos_e, sin_e, tables, cm_ws, bs4, lng3, lnb3, e, batch, seq)
            proj_s = _normproj(xs, mix_norm3, l, ev_w_in, e, nb, 512)
            mix_s, st_s, vn_s = _even_sample(proj_s.reshape(nb, 1, even_in), cos_es, sin_es, tables,
                                             state_ret, cm_ws, bs4, lng3, lnb3, e)
            xp = _matres(mix_p, ev_w_out, xp, e, bm_p, 512)
            xs = _matres(mix_s.reshape(nb, -1), ev_w_out, xs, e, nb, 512)
            ret_p.append(st_p)
            ret_s.append(st_s)
            cmv_p.append(vn_p)
            cmv_s.append(vn_s)
        else:
            o = l // 2
            lam_init = 0.8 - 0.6 * math.exp(-0.3 * l)
            qn = da_q_norm[o].reshape(1, DA_DH)
            kn = da_k_norm[o].reshape(1, DA_DH)
            lam_vecs = [a[o].reshape(1, DA_DH) for a in (da_lam_q1, da_lam_k1, da_lam_q2, da_lam_k2)]
            subln = da_subln[o].reshape(1, DA_VDIM)
            tb = seq // bm_p
            q_p, k_p, v_p = _normproj_qkv(xp, mix_norm3, l, od_w_in, o, bm_p, 512, BF16, qn, kn, c2_p, s2_p, tb)
            att_p = _flash_prompt(q_p, k_p, v_p, lam_vecs, subln, batch, seq, lam_init, tq=1024, tk=512)
            q_s, k_s, v_s = _normproj_qkv(xs, mix_norm3, l, od_w_in, o, nb, 512, F32, qn, kn, c2_s, s2_s, 1)
            v_rep = jnp.repeat(v_s.reshape(nb, DA_HEADS, DA_VDIM), 2, axis=1)
            att_s = _decode_attn(q_s.reshape(nb, 2 * DA_HEADS, DA_DH), k_s.reshape(nb, 2 * DA_HEADS, DA_DH), v_rep,
                                 cache_k, cache_v, page_table, lam_vecs, subln, o, lam_init, pages_per_step=4)
            xp = _matres(att_p, od_w_out, xp, o, bm_p, 512)
            xs = _matres(att_s.reshape(nb, d), od_w_out, xs, o, nb, 512)
            kp_l.append(k_p.reshape(batch, seq, 2 * DA_HEADS, DA_DH))
            vp_l.append(v_p.reshape(batch, seq, DA_HEADS, DA_VDIM))
            ks_l.append(k_s.reshape(nb, t_new, 2 * DA_HEADS, DA_DH))
            vs_l.append(v_s.reshape(nb, t_new, DA_HEADS, DA_VDIM))
        xp, xs = ffn_pair(xp, xs, ffn_b_norm3, ffn_b_w1, ffn_b_w3, ffn_b_w2, l)

    return (xp.reshape(batch, seq, d), xs.reshape(nb, t_new, d),
            jnp.stack(ret_p), jnp.stack(ret_s), jnp.stack(cmv_p), jnp.stack(cmv_s),
            jnp.stack(kp_l), jnp.stack(vp_l), jnp.stack(ks_l), jnp.stack(vs_l))
```

```python
import functools
import math

import numpy as np
import jax
import jax.numpy as jnp
from jax import lax
from jax.experimental import pallas as pl
from jax.experimental.pallas import tpu as pltpu

F32 = jnp.float32
BF16 = jnp.bfloat16

PAST_LEN = 16384
R_HEADS = 4
R_DK = 256
R_DV = 256
R_CHUNK = 128
CM_GROUPS = 4
CM_GDIM = 256
DA_HEADS = 8
DA_DH = 128
DA_VDIM = 256
LOG2_DA_DH = 7
LOG2_DA_VDIM = 8
ROPE_THETA = 10000.0
EPS = 1e-6
SUBLN_EPS = 1e-5
LN_EPS = 1e-5

BF16_SUBLANES = 16
VMEM_LIMIT_BYTES = 56 * 1024 * 1024


def _cparams(n_axes):
    return pltpu.CompilerParams(
        dimension_semantics=("arbitrary",) * n_axes,
        vmem_limit_bytes=VMEM_LIMIT_BYTES,
    )


def _silu(a):
    return a / (1.0 + jnp.exp(-a))


def _gelu_tanh(x):
    c = math.sqrt(2.0 / math.pi)
    return x * (0.5 * (1.0 + jnp.tanh(c * (x + 0.044715 * (x * x * x)))))


def _rms(x, eps):
    return x * lax.rsqrt(jnp.mean(x * x, axis=-1, keepdims=True) + eps)


def _dot(a, b):
    return jnp.dot(a, b, preferred_element_type=F32)


def _dot_nt(a, b):
    return lax.dot_general(a, b, (((1,), (1,)), ((), ())), preferred_element_type=F32)


def _ffn_body(x_ref, xs_ref, g_ref, w1_ref, w3_ref, w2_ref, o_ref, os_ref, xn_ref, *, n_split):
    i = pl.program_id(0)
    j = pl.program_id(1)
    bm = x_ref.shape[0]
    ms = xs_ref.shape[0]
    tail = xn_ref.shape[0] - bm

    @pl.when(j == 0)
    def _init():
        x = x_ref[...]
        xn_ref[0:bm, :] = (_rms(x, EPS) * g_ref[...]).astype(BF16)
        o_ref[...] = x

    @pl.when((i == 0) & (j == 0))
    def _init_sample():
        xs = xs_ref[...]
        xsn = jnp.concatenate([_rms(xs, EPS) * g_ref[...], jnp.zeros((tail - ms, xs.shape[1]), F32)], axis=0)
        xn_ref[bm:bm + tail, :] = xsn.astype(BF16)
        os_ref[...] = xs

    xn = xn_ref[...]
    a = _dot(xn, w1_ref[...].astype(BF16))
    b = _dot(xn, w3_ref[...].astype(BF16))
    h = ((_silu(a) * b) * 0.5).astype(BF16)
    d = o_ref.shape[-1]
    w = d // n_split
    for n in range(n_split):
        sl = slice(n * w, (n + 1) * w)
        y = _dot(h, w2_ref[:, sl].astype(BF16))
        o_ref[:, sl] += y[0:bm]
        os_ref[:, sl] += jnp.where(i == 0, y[bm:bm + ms], 0.0)


def _ffn(x, xs, norm3, w1, w3, w2, layer, bm, bf):
    m, d = x.shape
    ms = xs.shape[0]
    ff = w1.shape[-1]
    return pl.pallas_call(
        functools.partial(_ffn_body, n_split=4),
        grid=(m // bm, ff // bf),
        in_specs=[
            pl.BlockSpec((bm, d), lambda i, j: (i, 0), pipeline_mode=pl.Buffered(1)),
            pl.BlockSpec((ms, d), lambda i, j: (0, 0)),
            pl.BlockSpec((None, 1, d), lambda i, j: (layer, 0, 0)),
            pl.BlockSpec((None, d, bf), lambda i, j: (layer, 0, j)),
            pl.BlockSpec((None, d, bf), lambda i, j: (layer, 0, j)),
            pl.BlockSpec((None, bf, d), lambda i, j: (layer, j, 0)),
        ],
        out_specs=[
            pl.BlockSpec((bm, d), lambda i, j: (i, 0)),
            pl.BlockSpec((ms, d), lambda i, j: (0, 0)),
        ],
        out_shape=[jax.ShapeDtypeStruct((m, d), F32), jax.ShapeDtypeStruct((ms, d), F32)],
        scratch_shapes=[pltpu.VMEM((bm + BF16_SUBLANES, d), BF16)],
        compiler_params=_cparams(2),
        name="ffn",
    )(x, xs, norm3, w1, w3, w2)


def _normproj_body(x_ref, g_ref, w_ref, o_ref, xn_ref):
    @pl.when(pl.program_id(1) == 0)
    def _init():
        xn_ref[...] = (_rms(x_ref[...], EPS) * g_ref[...]).astype(BF16)

    o_ref[...] = _dot(xn_ref[...], w_ref[...].astype(BF16)).astype(o_ref.dtype)


def _normproj(x, norm3, norm_layer, w, layer, bm, bn):
    m, d = x.shape
    ncols = w.shape[-1]
    return pl.pallas_call(
        _normproj_body,
        grid=(m // bm, ncols // bn),
        in_specs=[
            pl.BlockSpec((bm, d), lambda i, j: (i, 0)),
            pl.BlockSpec((None, 1, d), lambda i, j: (norm_layer, 0, 0)),
            pl.BlockSpec((None, d, bn), lambda i, j: (layer, 0, j)),
        ],
        out_specs=pl.BlockSpec((bm, bn), lambda i, j: (i, j)),
        out_shape=jax.ShapeDtypeStruct((m, ncols), F32),
        scratch_shapes=[pltpu.VMEM((bm, d), BF16)],
        compiler_params=_cparams(2),
        name="normproj",
    )(x, norm3, w)


def _normproj_qkv_body(x_ref, g_ref, w_ref, qn_ref, kn_ref, c_ref, s_ref, q_ref, k_ref, v_ref, xn_ref, *, nj):
    j = pl.program_id(1)

    @pl.when(j == 0)
    def _init():
        xn_ref[...] = (_rms(x_ref[...], EPS) * g_ref[...]).astype(BF16)

    y = _dot(xn_ref[...], w_ref[...].astype(BF16))

    def norm_rope(hn_ref, o_ref):
        hn = hn_ref[...]
        c = c_ref[...]
        s = s_ref[...]
        for t in range(y.shape[-1] // DA_DH):
            sl = slice(t * DA_DH, (t + 1) * DA_DH)
            yn = _rms(y[:, sl], EPS) * hn
            o_ref[:, sl] = (yn * c + pltpu.roll(yn, DA_DH // 2, axis=1) * s).astype(o_ref.dtype)

    @pl.when(j < nj)
    def _q():
        norm_rope(qn_ref, q_ref)

    @pl.when((j >= nj) & (j < 2 * nj))
    def _k():
        norm_rope(kn_ref, k_ref)

    @pl.when(j >= 2 * nj)
    def _v():
        v_ref[...] = y


def _normproj_qkv(x, norm3, norm_layer, w, layer, bm, bn, q_dtype, qn, kn, c2, s2, table_blocks):
    m, d = x.shape
    nj = d // bn
    clip = lambda j, lo: jnp.clip(j - lo, 0, nj - 1)
    return pl.pallas_call(
        functools.partial(_normproj_qkv_body, nj=nj),
        grid=(m // bm, 3 * nj),
        in_specs=[
            pl.BlockSpec((bm, d), lambda i, j: (i, 0)),
            pl.BlockSpec((None, 1, d), lambda i, j: (norm_layer, 0, 0)),
            pl.BlockSpec((None, d, bn), lambda i, j: (layer, 0, j)),
            pl.BlockSpec((1, DA_DH), lambda i, j: (0, 0)),
            pl.BlockSpec((1, DA_DH), lambda i, j: (0, 0)),
            pl.BlockSpec((bm, DA_DH), lambda i, j: (i % table_blocks, 0)),
            pl.BlockSpec((bm, DA_DH), lambda i, j: (i % table_blocks, 0)),
        ],
        out_specs=[
            pl.BlockSpec((bm, bn), lambda i, j: (i, clip(j, 0))),
            pl.BlockSpec((bm, bn), lambda i, j: (i, clip(j, nj))),
            pl.BlockSpec((bm, bn), lambda i, j: (i, clip(j, 2 * nj))),
        ],
        out_shape=[
            jax.ShapeDtypeStruct((m, d), q_dtype),
            jax.ShapeDtypeStruct((m, d), F32),
            jax.ShapeDtypeStruct((m, d), F32),
        ],
        scratch_shapes=[pltpu.VMEM((bm, d), BF16)],
        compiler_params=_cparams(2),
        name="normproj_qkv",
    )(x, norm3, w, qn, kn, c2, s2)


def _matres_body(a_ref, w_ref, r_ref, o_ref):
    o_ref[...] = r_ref[...] + _dot(a_ref[...].astype(BF16), w_ref[...].astype(BF16))


def _matres(a, w, res, layer, bm, bn):
    m, k = a.shape
    n = w.shape[-1]
    return pl.pallas_call(
        _matres_body,
        grid=(m // bm, n // bn),
        in_specs=[
            pl.BlockSpec((bm, k), lambda i, j: (i, 0)),
            pl.BlockSpec((None, k, bn), lambda i, j: (layer, 0, j)),
            pl.BlockSpec((bm, bn), lambda i, j: (i, j)),
        ],
        out_specs=pl.BlockSpec((bm, bn), lambda i, j: (i, j)),
        out_shape=jax.ShapeDtypeStruct((m, n), F32),
        compiler_params=_cparams(2),
        name="matres",
    )(a, w, res)


def _retention_tables():
    expo = -5.0 - 7.0 * np.arange(R_HEADS, dtype=np.float64) / max(R_HEADS - 1, 1)
    log_g = np.log1p(-np.exp2(expo))
    idx = np.arange(R_CHUNK, dtype=np.float64)
    diff = idx[:, None] - idx[None, :]
    decay = np.where(diff >= 0, np.exp(log_g[:, None, None] * np.maximum(diff, 0.0)), 0.0)
    row_decay = np.exp(log_g[:, None] * (idx + 1.0))[:, :, None]
    k_decay = np.exp(log_g[:, None] * (R_CHUNK - 1.0 - idx))[:, :, None]
    state_decay = tuple(float(v) for v in np.exp(log_g * R_CHUNK))
    gamma = tuple(float(v) for v in np.exp(log_g))
    return (decay.astype(np.float32), row_decay.astype(np.float32), k_decay.astype(np.float32),
            state_decay, gamma)


def _rope_half(x, cos, sin):
    half = x.shape[-1] // 2
    x1 = x[:, :half]
    x2 = x[:, half:]
    return jnp.concatenate([x1 * cos - x2 * sin, x2 * cos + x1 * sin], axis=-1)


def _layer_norm(x, g, b):
    xc = x - jnp.mean(x, axis=-1, keepdims=True)
    return xc * lax.rsqrt(jnp.mean(xc * xc, axis=-1, keepdims=True) + LN_EPS) * g + b


def _even_prompt_body(p_ref, cos_ref, sin_ref, dec_ref, rd_ref, kd_ref, ws_ref, bs_ref, lng_ref, lnb_ref,
                      o_ref, st_ref, vn_ref, state, *, state_decay):
    c = pl.program_id(1)
    last = pl.num_programs(1) - 1

    @pl.when(c == 0)
    def _zero():
        state[...] = jnp.zeros_like(state)

    cos = cos_ref[...]
    sin = sin_ref[...]
    qk_w = R_HEADS * R_DK
    for h in range(R_HEADS):
        q = _rope_half(p_ref[:, h * R_DK:(h + 1) * R_DK], cos, sin)
        k = _rope_half(p_ref[:, qk_w + h * R_DK:qk_w + (h + 1) * R_DK], cos, sin) * (R_DK ** -0.5)
        v = p_ref[:, 2 * qk_w + h * R_DV:2 * qk_w + (h + 1) * R_DV].astype(BF16)
        g = p_ref[:, 3 * qk_w + h * R_DV:3 * qk_w + (h + 1) * R_DV]
        qb = q.astype(BF16)
        s = _dot_nt(qb, k.astype(BF16)) * dec_ref[h]
        st = state[h]
        o = _dot(s.astype(BF16), v) + _dot(qb, st.astype(BF16)) * rd_ref[h]
        kd_t = (k * kd_ref[h]).T.astype(BF16)
        state[h] = st * state_decay[h] + _dot(kd_t, v)
        o_ref[:, h * R_DV:(h + 1) * R_DV] = (_silu(g) * _rms(o, EPS)).astype(o_ref.dtype)

    u0 = 3 * qk_w + R_HEADS * R_DV
    cmw = CM_GROUPS * CM_GDIM
    vn = _layer_norm(_gelu_tanh(p_ref[:, u0 + cmw:u0 + 2 * cmw]), lng_ref[...], lnb_ref[...])
    row = lax.broadcasted_iota(jnp.int32, (R_CHUNK, R_CHUNK), 0)
    col = lax.broadcasted_iota(jnp.int32, (R_CHUNK, R_CHUNK), 1)
    for gi in range(CM_GROUPS):
        sl = slice(gi * CM_GDIM, (gi + 1) * CM_GDIM)
        w = jnp.where(row >= col, ws_ref[gi], 0.0).astype(BF16)
        mixed = _dot(w, vn[:, sl].astype(BF16)) + bs_ref[gi]
        u = p_ref[:, u0 + gi * CM_GDIM:u0 + (gi + 1) * CM_GDIM]
        o_ref[:, R_HEADS * R_DV + gi * CM_GDIM:R_HEADS * R_DV + (gi + 1) * CM_GDIM] = (
            _gelu_tanh(u) * mixed).astype(o_ref.dtype)

    @pl.when(c == last)
    def _emit():
        st_ref[...] = state[...]
        vn_ref[...] = vn


def _even_prompt(proj, cos, sin, tables, ws, bs4, lng3, lnb3, e, batch, seq):
    decay, row_decay, k_decay, state_decay, _ = tables
    nc = seq // R_CHUNK
    width = proj.shape[-1]
    cmw = CM_GROUPS * CM_GDIM
    out_w = R_HEADS * R_DV + cmw
    const3 = lambda b, c: (0, 0, 0)
    return pl.pallas_call(
        functools.partial(_even_prompt_body, state_decay=state_decay),
        grid=(batch, nc),
        in_specs=[
            pl.BlockSpec((R_CHUNK, width), lambda b, c: (b * nc + c, 0)),
            pl.BlockSpec((R_CHUNK, R_DK // 2), lambda b, c: (c, 0)),
            pl.BlockSpec((R_CHUNK, R_DK // 2), lambda b, c: (c, 0)),
            pl.BlockSpec(decay.shape, const3),
            pl.BlockSpec(row_decay.shape, const3),
            pl.BlockSpec(k_decay.shape, const3),
            pl.BlockSpec((None, CM_GROUPS, R_CHUNK, R_CHUNK), lambda b, c: (e, 0, 0, 0)),
            pl.BlockSpec((None, CM_GROUPS, R_CHUNK, 1), lambda b, c: (e, 0, 0, 0)),
            pl.BlockSpec((None, 1, cmw), lambda b, c: (e, 0, 0)),
            pl.BlockSpec((None, 1, cmw), lambda b, c: (e, 0, 0)),
        ],
        out_specs=[
            pl.BlockSpec((R_CHUNK, out_w), lambda b, c: (b * nc + c, 0)),
            pl.BlockSpec((None, R_HEADS, R_DK, R_DV), lambda b, c: (b, 0, 0, 0)),
            pl.BlockSpec((None, R_CHUNK, cmw), lambda b, c: (b, 0, 0)),
        ],
        out_shape=[
            jax.ShapeDtypeStruct((batch * seq, out_w), BF16),
            jax.ShapeDtypeStruct((batch, R_HEADS, R_DK, R_DV), F32),
            jax.ShapeDtypeStruct((batch, R_CHUNK, cmw), F32),
        ],
        scratch_shapes=[pltpu.VMEM((R_HEADS, R_DK, R_DV), F32)],
        compiler_params=_cparams(2),
        name="even_prompt",
    )(proj, cos, sin, jnp.asarray(decay), jnp.asarray(row_decay), jnp.asarray(k_decay), ws, bs4, lng3, lnb3)


def _even_sample_body(p_ref, cos_ref, sin_ref, st_in_ref, ws_ref, bs_ref, lng_ref, lnb_ref,
                      o_ref, st_ref, vn_ref, *, gamma):
    cos = cos_ref[...]
    sin = sin_ref[...]
    qk_w = R_HEADS * R_DK
    r0 = lax.broadcasted_iota(jnp.int32, (R_DK, R_DK), 0)
    r1 = lax.broadcasted_iota(jnp.int32, (R_DK, R_DK), 1)
    eye = jnp.where(r0 == r1, 1.0, 0.0).astype(BF16)
    for h in range(R_HEADS):
        q = _rope_half(p_ref[:, h * R_DK:(h + 1) * R_DK], cos, sin)
        k = _rope_half(p_ref[:, qk_w + h * R_DK:qk_w + (h + 1) * R_DK], cos, sin) * (R_DK ** -0.5)
        v = p_ref[:, 2 * qk_w + h * R_DV:2 * qk_w + (h + 1) * R_DV]
        g = p_ref[:, 3 * qk_w + h * R_DV:3 * qk_w + (h + 1) * R_DV]
        st = st_in_ref[h]
        s = jnp.sum(q * k, axis=-1, keepdims=True)
        qb = jnp.broadcast_to(q, (8, R_DK)).astype(BF16)
        cross = _dot(qb, st.astype(BF16))[0:1, :]
        o = s * v + cross * gamma[h]
        kb = jnp.broadcast_to(k, (128, R_DK)).astype(BF16)
        kcol = _dot_nt(eye, kb)
        kcol = jnp.concatenate([kcol] * (R_DV // 128), axis=-1)
        vb = v.astype(BF16).astype(F32)
        st_ref[h] = st * gamma[h] + kcol * vb
        o_ref[:, h * R_DV:(h + 1) * R_DV] = (_silu(g) * _rms(o, EPS)).astype(o_ref.dtype)

    u0 = 3 * qk_w + R_HEADS * R_DV
    cmw = CM_GROUPS * CM_GDIM
    vn = _layer_norm(_gelu_tanh(p_ref[:, u0 + cmw:u0 + 2 * cmw]), lng_ref[...], lnb_ref[...])
    vn_ref[...] = vn
    for gi in range(CM_GROUPS):
        sl = slice(gi * CM_GDIM, (gi + 1) * CM_GDIM)
        mixed = ws_ref[gi, 0:1, 0:1] * vn[:, sl] + bs_ref[gi, 0:1, :]
        u = p_ref[:, u0 + gi * CM_GDIM:u0 + (gi + 1) * CM_GDIM]
        o_ref[:, R_HEADS * R_DV + gi * CM_GDIM:R_HEADS * R_DV + (gi + 1) * CM_GDIM] = (
            _gelu_tanh(u) * mixed).astype(o_ref.dtype)


def _even_sample(proj3, cos, sin, tables, st_in, ws, bs4, lng3, lnb3, e):
    gamma = tables[4]
    nb = proj3.shape[0]
    width = proj3.shape[-1]
    cmw = CM_GROUPS * CM_GDIM
    out_w = R_HEADS * R_DV + cmw
    return pl.pallas_call(
        functools.partial(_even_sample_body, gamma=gamma),
        grid=(nb,),
        in_specs=[
            pl.BlockSpec((None, 1, width), lambda b: (b, 0, 0)),
            pl.BlockSpec((1, R_DK // 2), lambda b: (0, 0)),
            pl.BlockSpec((1, R_DK // 2), lambda b: (0, 0)),
            pl.BlockSpec((None, None, R_HEADS, R_DK, R_DV), lambda b: (e, b, 0, 0, 0)),
            pl.BlockSpec((None, CM_GROUPS, R_CHUNK, R_CHUNK), lambda b: (e, 0, 0, 0)),
            pl.BlockSpec((None, CM_GROUPS, R_CHUNK, 1), lambda b: (e, 0, 0, 0)),
            pl.BlockSpec((None, 1, cmw), lambda b: (e, 0, 0)),
            pl.BlockSpec((None, 1, cmw), lambda b: (e, 0, 0)),
        ],
        out_specs=[
            pl.BlockSpec((None, 1, out_w), lambda b: (b, 0, 0)),
            pl.BlockSpec((None, R_HEADS, R_DK, R_DV), lambda b: (b, 0, 0, 0)),
            pl.BlockSpec((None, 1, cmw), lambda b: (b, 0, 0)),
        ],
        out_shape=[
            jax.ShapeDtypeStruct((nb, 1, out_w), F32),
            jax.ShapeDtypeStruct((nb, R_HEADS, R_DK, R_DV), F32),
            jax.ShapeDtypeStruct((nb, 1, cmw), F32),
        ],
        compiler_params=_cparams(1),
        name="even_sample",
    )(proj3, cos, sin, st_in, ws, bs4, lng3, lnb3)


def _diff_lambda(lq1_ref, lk1_ref, lq2_ref, lk2_ref, lam_init):
    a = jnp.sum(lq1_ref[...] * lk1_ref[...], axis=-1, keepdims=True)
    b = jnp.sum(lq2_ref[...] * lk2_ref[...], axis=-1, keepdims=True)
    return jnp.exp(a) - jnp.exp(b) + lam_init


def _flash_body(qi_tab, ki_tab, q_ref, k_ref, v_ref, lq1_ref, lk1_ref, lq2_ref, lk2_ref, sub_ref,
                o_ref, m_ref, l_ref, acc_ref, *, scale, lam_init):
    t = pl.program_id(2)
    qi = qi_tab[t]
    ki = ki_tab[t]
    tq = q_ref.shape[0]
    tk = k_ref.shape[0]
    r = tq // tk
    c = scale * math.log2(math.e)

    @pl.when(ki == 0)
    def _init():
        m_ref[...] = jnp.full_like(m_ref, -jnp.inf)
        l_ref[...] = jnp.zeros_like(l_ref)
        acc_ref[...] = jnp.zeros_like(acc_ref)

    def update(modes):
        vb = v_ref[...].astype(BF16)
        ones = jnp.ones((tk, DA_DH), BF16)
        if "tri" in modes:
            row = lax.broadcasted_iota(jnp.int32, (tk, tk), 0)
            col = lax.broadcasted_iota(jnp.int32, (tk, tk), 1)
            keep = col <= row
        for half in range(2):
            sl = slice(half * DA_DH, (half + 1) * DA_DH)
            kb = k_ref[:, sl].astype(BF16)
            for rb, mode in enumerate(modes):
                if mode == "skip":
                    continue
                rows = slice(rb * tk, (rb + 1) * tk)
                s = _dot_nt(q_ref[rows, sl], kb)
                if mode == "tri":
                    s = jnp.where(keep, s, -jnp.inf)
                m_old = m_ref[half, rows]
                m_new = jnp.maximum(m_old, jnp.max(s, axis=-1, keepdims=True))
                alpha = jnp.exp2(c * (m_old - m_new))
                p = jnp.exp2((c * s - c * m_new).astype(BF16))
                l_ref[half, rows] = alpha * l_ref[half, rows] + _dot(p, ones)
                acc_ref[half, rows] = alpha * acc_ref[half, rows] + _dot(p, vb)
                m_ref[half, rows] = m_new

    @pl.when(ki < qi * r)
    def _below_diag():
        update(["full"] * r)

    for kd in range(r):
        @pl.when(ki == qi * r + kd)
        def _on_diag(kd=kd):
            update(["skip"] * kd + ["tri"] + ["full"] * (r - 1 - kd))

    @pl.when(ki == qi * r + (r - 1))
    def _finish():
        lam = _diff_lambda(lq1_ref, lk1_ref, lq2_ref, lk2_ref, lam_init)
        reps = DA_VDIM // DA_DH
        inv0 = jnp.concatenate([1.0 / l_ref[0]] * reps, axis=-1)
        inv1 = jnp.concatenate([1.0 / l_ref[1]] * reps, axis=-1)
        o = acc_ref[0] * inv0 - lam * (acc_ref[1] * inv1)
        o_ref[...] = (_rms(o, SUBLN_EPS) * sub_ref[...] * (1.0 - lam_init)).astype(o_ref.dtype)


def _flash_prompt(q, k, v, lam_vecs, subln, batch, seq, lam_init, tq, tk):
    nq = seq // tq
    r = tq // tk
    qi_tab = np.concatenate([np.full((i + 1) * r, i, np.int32) for i in range(nq)])
    ki_tab = np.concatenate([np.arange((i + 1) * r, dtype=np.int32) for i in range(nq)])
    n_tri = int(qi_tab.shape[0])
    vec_spec = pl.BlockSpec((1, DA_DH), lambda b, h, t, qt, kt: (0, 0))
    grid_spec = pltpu.PrefetchScalarGridSpec(
        num_scalar_prefetch=2,
        grid=(batch, DA_HEADS, n_tri),
        in_specs=[
            pl.BlockSpec((tq, DA_VDIM), lambda b, h, t, qt, kt: (b * nq + qt[t], h)),
            pl.BlockSpec((tk, DA_VDIM), lambda b, h, t, qt, kt: (b * nq * r + kt[t], h)),
            pl.BlockSpec((tk, DA_VDIM), lambda b, h, t, qt, kt: (b * nq * r + kt[t], h)),
            vec_spec, vec_spec, vec_spec, vec_spec,
            pl.BlockSpec((1, DA_VDIM), lambda b, h, t, qt, kt: (0, 0)),
        ],
        out_specs=pl.BlockSpec((tq, DA_VDIM), lambda b, h, t, qt, kt: (b * nq + qt[t], h)),
        scratch_shapes=[
            pltpu.VMEM((2, tq, 1), F32),
            pltpu.VMEM((2, tq, DA_DH), F32),
            pltpu.VMEM((2, tq, DA_VDIM), F32),
        ],
    )
    return pl.pallas_call(
        functools.partial(_flash_body, scale=DA_DH ** -0.5, lam_init=lam_init),
        grid_spec=grid_spec,
        out_shape=jax.ShapeDtypeStruct((batch * seq, DA_HEADS * DA_VDIM), BF16),
        compiler_params=_cparams(3),
        name="diff_attn_prompt",
    )(jnp.asarray(qi_tab), jnp.asarray(ki_tab), q, k, v, *lam_vecs, subln)


def _decode_body(pt_ref, q_ref, kn_ref, vn_ref, *rest, scale, lam_init, pages_per_step):
    k_refs = rest[:pages_per_step]
    v_refs = rest[pages_per_step:2 * pages_per_step]
    (exp_ref, own_ref, lq1_ref, lk1_ref, lq2_ref, lk2_ref, sub_ref,
     o_ref, m_ref, l_ref, acc_ref) = rest[2 * pages_per_step:]
    step = pl.program_id(1)
    n_hh = 2 * DA_HEADS
    page = k_refs[0].shape[0] // n_hh
    t_len = pages_per_step * page

    @pl.when(step == 0)
    def _init():
        m_ref[...] = jnp.full_like(m_ref, -jnp.inf)
        l_ref[...] = jnp.zeros_like(l_ref)
        acc_ref[...] = jnp.zeros_like(acc_ref)

    qb = q_ref[...].astype(BF16)
    row_t = lax.broadcasted_iota(jnp.int32, (n_hh, t_len), 0)
    s = jnp.zeros((n_hh, t_len), F32)
    for hh in range(n_hh):
        k_hh = jnp.concatenate([kr[pl.ds(hh, page, stride=n_hh), :] for kr in k_refs], axis=0).astype(BF16)
        s = jnp.where(row_t == hh, _dot_nt(qb, k_hh), s)
    s = s * scale
    m_old = m_ref[...]
    m_new = jnp.maximum(m_old, jnp.max(s, axis=-1, keepdims=True))
    alpha = jnp.exp(m_old - m_new)
    p = jnp.exp(s - m_new)
    l_ref[...] = alpha * l_ref[...] + jnp.sum(p, axis=-1, keepdims=True)
    pb = p.astype(BF16)
    pv = jnp.zeros((n_hh, DA_VDIM), F32)
    for r, vr in enumerate(v_refs):
        spread = _dot(pb[:, r * page:(r + 1) * page], exp_ref[...]) * own_ref[...]
        pv += _dot(spread.astype(BF16), vr[...].astype(BF16))
    acc_ref[...] = alpha * acc_ref[...] + pv
    m_ref[...] = m_new

    @pl.when(step == pl.num_programs(1) - 1)
    def _finish():
        s_new = jnp.sum(q_ref[...] * kn_ref[...], axis=-1, keepdims=True) * scale
        m_old2 = m_ref[...]
        m_fin = jnp.maximum(m_old2, s_new)
        alpha2 = jnp.exp(m_old2 - m_fin)
        p_new = jnp.exp(s_new - m_fin)
        l_fin = alpha2 * l_ref[...] + p_new
        a = (alpha2 * acc_ref[...] + p_new * vn_ref[...]) / l_fin
        lam = _diff_lambda(lq1_ref, lk1_ref, lq2_ref, lk2_ref, lam_init)
        for h in range(DA_HEADS):
            o = a[2 * h:2 * h + 1, :] - lam * a[2 * h + 1:2 * h + 2, :]
            o_ref[:, h * DA_VDIM:(h + 1) * DA_VDIM] = (
                _rms(o, SUBLN_EPS) * sub_ref[...] * (1.0 - lam_init)).astype(o_ref.dtype)


def _decode_attn(q3, kn3, vn3, cache_k, cache_v, page_table, lam_vecs, subln, layer, lam_init, pages_per_step):
    nb = q3.shape[0]
    n_hh = 2 * DA_HEADS
    n_layers, n_pool, page = cache_k.shape[:3]
    n_pages = page_table.shape[1]
    n_steps = n_pages // pages_per_step
    cache_k = cache_k.reshape(n_layers, n_pool, page * n_hh, DA_DH)
    cache_v = cache_v.reshape(n_layers, n_pool, page * DA_HEADS, DA_VDIM)

    def page_spec(r, heads, dim):
        return pl.BlockSpec((None, None, page * heads, dim),
                            lambda b, s, pt: (layer, pt[b, s * pages_per_step + r], 0, 0))

    rows_v = np.arange(page * DA_HEADS)
    spread = (rows_v[None, :] // DA_HEADS == np.arange(page)[:, None]).astype(np.float32)
    own = (rows_v[None, :] % DA_HEADS == np.arange(n_hh)[:, None] // 2).astype(np.float32)

    vec_spec = pl.BlockSpec((1, DA_DH), lambda b, s, pt: (0, 0))
    grid_spec = pltpu.PrefetchScalarGridSpec(
        num_scalar_prefetch=1,
        grid=(nb, n_steps),
        in_specs=[pl.BlockSpec((None, n_hh, DA_DH), lambda b, s, pt: (b, 0, 0)),
                  pl.BlockSpec((None, n_hh, DA_DH), lambda b, s, pt: (b, 0, 0)),
                  pl.BlockSpec((None, n_hh, DA_VDIM), lambda b, s, pt: (b, 0, 0))]
        + [page_spec(r, n_hh, DA_DH) for r in range(pages_per_step)]
        + [page_spec(r, DA_HEADS, DA_VDIM) for r in range(pages_per_step)]
        + [pl.BlockSpec(spread.shape, lambda b, s, pt: (0, 0)),
           pl.BlockSpec(own.shape, lambda b, s, pt: (0, 0))]
        + [vec_spec, vec_spec, vec_spec, vec_spec,
           pl.BlockSpec((1, DA_VDIM), lambda b, s, pt: (0, 0))],
        out_specs=pl.BlockSpec((None, 1, DA_HEADS * DA_VDIM), lambda b, s, pt: (b, 0, 0)),
        scratch_shapes=[
            pltpu.VMEM((n_hh, 1), F32),
            pltpu.VMEM((n_hh, 1), F32),
            pltpu.VMEM((n_hh, DA_VDIM), F32),
        ],
    )
    return pl.pallas_call(
        functools.partial(_decode_body, scale=DA_DH ** -0.5, lam_init=lam_init, pages_per_step=pages_per_step),
        grid_spec=grid_spec,
        out_shape=jax.ShapeDtypeStruct((nb, 1, DA_HEADS * DA_VDIM), F32),
        compiler_params=_cparams(2),
        name="diff_attn_decode",
    )(page_table, q3, kn3, vn3, *([cache_k] * pages_per_step), *([cache_v] * pages_per_step),
      jnp.asarray(spread, BF16), jnp.asarray(own), *lam_vecs, subln)


def _rope_tables(pos, d):
    inv = ROPE_THETA ** (-jnp.arange(0, d, 2, dtype=F32) / d)
    ang = pos.astype(F32)[:, None] * inv[None, :]
    return jnp.cos(ang), jnp.sin(ang)


def kernel(x_prompt, x_sample, state_ret, cache_k, cache_v, page_table, ffn_a_norm, ffn_a_w1, ffn_a_w3, ffn_a_w2, mix_norm, ffn_b_norm, ffn_b_w1, ffn_b_w3, ffn_b_w2, ev_w_in, ev_w_out, cm_ws, cm_bs, cm_ln_g, cm_ln_b, od_w_in, od_w_out, da_q_norm, da_k_norm, da_lam_q1, da_lam_k1, da_lam_q2, da_lam_k2, da_subln):
    batch, seq, d = x_prompt.shape
    nb, t_new, _ = x_sample.shape
    depth = ffn_a_norm.shape[0]
    assert t_new == 1 and seq % 1024 == 0 and nb == 8

    bm_p = 1024
    pos_p = jnp.arange(seq, dtype=jnp.int32)
    pos_s = PAST_LEN + jnp.arange(t_new, dtype=jnp.int32)
    cos_e, sin_e = _rope_tables(pos_p, R_DK)
    cos_es, sin_es = _rope_tables(pos_s, R_DK)
    cos_o, sin_o = _rope_tables(pos_p, DA_DH)
    cos_os, sin_os = _rope_tables(pos_s, DA_DH)
    c2_p = jnp.concatenate([cos_o, cos_o], axis=-1)
    s2_p = jnp.concatenate([-sin_o, sin_o], axis=-1)
    c2_s = jnp.broadcast_to(jnp.concatenate([cos_os, cos_os], axis=-1), (nb, DA_DH))
    s2_s = jnp.broadcast_to(jnp.concatenate([-sin_os, sin_os], axis=-1), (nb, DA_DH))
    tables = _retention_tables()

    as3 = lambda a: a.reshape(a.shape[0], 1, a.shape[1])
    ffn_a_norm3, ffn_b_norm3, mix_norm3 = as3(ffn_a_norm), as3(ffn_b_norm), as3(mix_norm)
    lng3, lnb3 = as3(cm_ln_g), as3(cm_ln_b)
    bs4 = cm_bs.reshape(cm_bs.shape + (1,))

    xp = x_prompt.reshape(batch * seq, d)
    xs = x_sample.reshape(nb * t_new, d)
    ret_p, ret_s, cmv_p, cmv_s = [], [], [], []
    kp_l, vp_l, ks_l, vs_l = [], [], [], []

    def ffn_pair(xp, xs, norm3, w1, w3, w2, l):
        return _ffn(xp, xs, norm3, w1, w3, w2, l, bm=bm_p, bf=256)

    for l in range(depth):
        xp, xs = ffn_pair(xp, xs, ffn_a_norm3, ffn_a_w1, ffn_a_w3, ffn_a_w2, l)
        if l % 2 == 0:
            e = l // 2
            even_in = ev_w_in.shape[-1]
            proj_p = _normproj(xp, mix_norm3, l, ev_w_in, e, bm_p, 1024)
            mix_p, st_p, vn_p = _even_prompt(proj_p, cos_e, sin_e, tables, cm_ws, bs4, lng3, lnb3, e, batch, seq)
            proj_s = _normproj(xs, mix_norm3, l, ev_w_in, e, nb, 1024)
            mix_s, st_s, vn_s = _even_sample(proj_s.reshape(nb, 1, even_in), cos_es, sin_es, tables,
                                             state_ret, cm_ws, bs4, lng3, lnb3, e)
            xp = _matres(mix_p, ev_w_out, xp, e, bm_p, 1024)
            xs = _matres(mix_s.reshape(nb, -1), ev_w_out, xs, e, nb, 512)
            ret_p.append(st_p)
            ret_s.append(st_s)
            cmv_p.append(vn_p)
            cmv_s.append(vn_s)
        else:
            o = l // 2
            lam_init = 0.8 - 0.6 * math.exp(-0.3 * l)
            qn = da_q_norm[o].reshape(1, DA_DH)
            kn = da_k_norm[o].reshape(1, DA_DH)
            lam_vecs = [a[o].reshape(1, DA_DH) for a in (da_lam_q1, da_lam_k1, da_lam_q2, da_lam_k2)]
            subln = da_subln[o].reshape(1, DA_VDIM)
            tb = seq // bm_p
            q_p, k_p, v_p = _normproj_qkv(xp, mix_norm3, l, od_w_in, o, bm_p, 512, BF16, qn, kn, c2_p, s2_p, tb)
            att_p = _flash_prompt(q_p, k_p, v_p, lam_vecs, subln, batch, seq, lam_init, tq=1024, tk=512)
            q_s, k_s, v_s = _normproj_qkv(xs, mix_norm3, l, od_w_in, o, nb, 512, F32, qn, kn, c2_s, s2_s, 1)
            v_rep = jnp.repeat(v_s.reshape(nb, DA_HEADS, DA_VDIM), 2, axis=1)
            att_s = _decode_attn(q_s.reshape(nb, 2 * DA_HEADS, DA_DH), k_s.reshape(nb, 2 * DA_HEADS, DA_DH), v_rep,
                                 cache_k, cache_v, page_table, lam_vecs, subln, o, lam_init, pages_per_step=8)
            xp = _matres(att_p, od_w_out, xp, o, bm_p, 1024)
            xs = _matres(att_s.reshape(nb, d), od_w_out, xs, o, nb, 512)
            kp_l.append(k_p.reshape(batch, seq, 2 * DA_HEADS, DA_DH))
            vp_l.append(v_p.reshape(batch, seq, DA_HEADS, DA_VDIM))
            ks_l.append(k_s.reshape(nb, t_new, 2 * DA_HEADS, DA_DH))
            vs_l.append(v_s.reshape(nb, t_new, DA_HEADS, DA_VDIM))
        xp, xs = ffn_pair(xp, xs, ffn_b_norm3, ffn_b_w1, ffn_b_w3, ffn_b_w2, l)

    return (xp.reshape(batch, seq, d), xs.reshape(nb, t_new, d),
            jnp.stack(ret_p), jnp.stack(ret_s), jnp.stack(cmv_p), jnp.stack(cmv_s),
            jnp.stack(kp_l), jnp.stack(vp_l), jnp.stack(ks_l), jnp.stack(vs_l))
```

```python
import functools
import math

import numpy as np
import jax
import jax.numpy as jnp
from jax import lax
from jax.experimental import pallas as pl
from jax.experimental.pallas import tpu as pltpu

F32 = jnp.float32
BF16 = jnp.bfloat16

PAST_LEN = 16384
R_HEADS = 4
R_DK = 256
R_DV = 256
R_CHUNK = 128
CM_GROUPS = 4
CM_GDIM = 256
DA_HEADS = 8
DA_DH = 128
DA_VDIM = 256
LOG2_DA_DH = 7
LOG2_DA_VDIM = 8
ROPE_THETA = 10000.0
EPS = 1e-6
SUBLN_EPS = 1e-5
LN_EPS = 1e-5

BF16_SUBLANES = 16
VMEM_LIMIT_BYTES = 56 * 1024 * 1024


def _cparams(n_axes):
    return pltpu.CompilerParams(
        dimension_semantics=("arbitrary",) * n_axes,
        vmem_limit_bytes=VMEM_LIMIT_BYTES,
    )


def _silu(a):
    return a / (1.0 + jnp.exp(-a))


def _gelu_tanh(x):
    c = math.sqrt(2.0 / math.pi)
    return x * (0.5 * (1.0 + jnp.tanh(c * (x + 0.044715 * (x * x * x)))))


def _rms(x, eps):
    return x * lax.rsqrt(jnp.mean(x * x, axis=-1, keepdims=True) + eps)


def _dot(a, b):
    return jnp.dot(a, b, preferred_element_type=F32)


def _dot_nt(a, b):
    return lax.dot_general(a, b, (((1,), (1,)), ((), ())), preferred_element_type=F32)


def _ffn_body(x_ref, xs_ref, g_ref, w1_ref, w3_ref, w2_ref, o_ref, os_ref, xn_ref, *, n_split):
    i = pl.program_id(0)
    j = pl.program_id(1)
    bm = x_ref.shape[0]
    ms = xs_ref.shape[0]
    tail = xn_ref.shape[0] - bm

    @pl.when(j == 0)
    def _init():
        x = x_ref[...]
        xn_ref[0:bm, :] = (_rms(x, EPS) * g_ref[...]).astype(BF16)
        o_ref[...] = x

    @pl.when((i == 0) & (j == 0))
    def _init_sample():
        xs = xs_ref[...]
        xsn = jnp.concatenate([_rms(xs, EPS) * g_ref[...], jnp.zeros((tail - ms, xs.shape[1]), F32)], axis=0)
        xn_ref[bm:bm + tail, :] = xsn.astype(BF16)
        os_ref[...] = xs

    xn = xn_ref[...]
    a = _dot(xn, w1_ref[...].astype(BF16))
    b = _dot(xn, w3_ref[...].astype(BF16))
    h = ((_silu(a) * b) * 0.5).astype(BF16)
    d = o_ref.shape[-1]
    w = d // n_split
    for n in range(n_split):
        sl = slice(n * w, (n + 1) * w)
        y = _dot(h, w2_ref[:, sl].astype(BF16))
        o_ref[:, sl] += y[0:bm]
        os_ref[:, sl] += jnp.where(i == 0, y[bm:bm + ms], 0.0)


def _ffn(x, xs, norm3, w1, w3, w2, layer, bm, bf):
    m, d = x.shape
    ms = xs.shape[0]
    ff = w1.shape[-1]
    return pl.pallas_call(
        functools.partial(_ffn_body, n_split=4),
        grid=(m // bm, ff // bf),
        in_specs=[
            pl.BlockSpec((bm, d), lambda i, j: (i, 0), pipeline_mode=pl.Buffered(1)),
            pl.BlockSpec((ms, d), lambda i, j: (0, 0)),
            pl.BlockSpec((None, 1, d), lambda i, j: (layer, 0, 0)),
            pl.BlockSpec((None, d, bf), lambda i, j: (layer, 0, j)),
            pl.BlockSpec((None, d, bf), lambda i, j: (layer, 0, j)),
            pl.BlockSpec((None, bf, d), lambda i, j: (layer, j, 0)),
        ],
        out_specs=[
            pl.BlockSpec((bm, d), lambda i, j: (i, 0)),
            pl.BlockSpec((ms, d), lambda i, j: (0, 0)),
        ],
        out_shape=[jax.ShapeDtypeStruct((m, d), F32), jax.ShapeDtypeStruct((ms, d), F32)],
        scratch_shapes=[pltpu.VMEM((bm + BF16_SUBLANES, d), BF16)],
        compiler_params=_cparams(2),
        name="ffn",
    )(x, xs, norm3, w1, w3, w2)


def _normproj_body(x_ref, g_ref, w_ref, o_ref, xn_ref):
    @pl.when(pl.program_id(1) == 0)
    def _init():
        xn_ref[...] = (_rms(x_ref[...], EPS) * g_ref[...]).astype(BF16)

    o_ref[...] = _dot(xn_ref[...], w_ref[...].astype(BF16)).astype(o_ref.dtype)


def _normproj(x, norm3, norm_layer, w, layer, bm, bn):
    m, d = x.shape
    ncols = w.shape[-1]
    return pl.pallas_call(
        _normproj_body,
        grid=(m // bm, ncols // bn),
        in_specs=[
            pl.BlockSpec((bm, d), lambda i, j: (i, 0)),
            pl.BlockSpec((None, 1, d), lambda i, j: (norm_layer, 0, 0)),
            pl.BlockSpec((None, d, bn), lambda i, j: (layer, 0, j)),
        ],
        out_specs=pl.BlockSpec((bm, bn), lambda i, j: (i, j)),
        out_shape=jax.ShapeDtypeStruct((m, ncols), F32),
        scratch_shapes=[pltpu.VMEM((bm, d), BF16)],
        compiler_params=_cparams(2),
        name="normproj",
    )(x, norm3, w)


def _normproj_qkv_body(x_ref, g_ref, w_ref, qn_ref, kn_ref, c_ref, s_ref, q_ref, k_ref, v_ref, xn_ref, *, nj):
    j = pl.program_id(1)

    @pl.when(j == 0)
    def _init():
        xn_ref[...] = (_rms(x_ref[...], EPS) * g_ref[...]).astype(BF16)

    y = _dot(xn_ref[...], w_ref[...].astype(BF16))

    def norm_rope(hn_ref, o_ref):
        hn = hn_ref[...]
        c = c_ref[...]
        s = s_ref[...]
        for t in range(y.shape[-1] // DA_DH):
            sl = slice(t * DA_DH, (t + 1) * DA_DH)
            yn = _rms(y[:, sl], EPS) * hn
            o_ref[:, sl] = (yn * c + pltpu.roll(yn, DA_DH // 2, axis=1) * s).astype(o_ref.dtype)

    @pl.when(j < nj)
    def _q():
        norm_rope(qn_ref, q_ref)

    @pl.when((j >= nj) & (j < 2 * nj))
    def _k():
        norm_rope(kn_ref, k_ref)

    @pl.when(j >= 2 * nj)
    def _v():
        v_ref[...] = y


def _normproj_qkv(x, norm3, norm_layer, w, layer, bm, bn, q_dtype, qn, kn, c2, s2, table_blocks):
    m, d = x.shape
    nj = d // bn
    clip = lambda j, lo: jnp.clip(j - lo, 0, nj - 1)
    return pl.pallas_call(
        functools.partial(_normproj_qkv_body, nj=nj),
        grid=(m // bm, 3 * nj),
        in_specs=[
            pl.BlockSpec((bm, d), lambda i, j: (i, 0)),
            pl.BlockSpec((None, 1, d), lambda i, j: (norm_layer, 0, 0)),
            pl.BlockSpec((None, d, bn), lambda i, j: (layer, 0, j)),
            pl.BlockSpec((1, DA_DH), lambda i, j: (0, 0)),
            pl.BlockSpec((1, DA_DH), lambda i, j: (0, 0)),
            pl.BlockSpec((bm, DA_DH), lambda i, j: (i % table_blocks, 0)),
            pl.BlockSpec((bm, DA_DH), lambda i, j: (i % table_blocks, 0)),
        ],
        out_specs=[
            pl.BlockSpec((bm, bn), lambda i, j: (i, clip(j, 0))),
            pl.BlockSpec((bm, bn), lambda i, j: (i, clip(j, nj))),
            pl.BlockSpec((bm, bn), lambda i, j: (i, clip(j, 2 * nj))),
        ],
        out_shape=[
            jax.ShapeDtypeStruct((m, d), q_dtype),
            jax.ShapeDtypeStruct((m, d), F32),
            jax.ShapeDtypeStruct((m, d), F32),
        ],
        scratch_shapes=[pltpu.VMEM((bm, d), BF16)],
        compiler_params=_cparams(2),
        name="normproj_qkv",
    )(x, norm3, w, qn, kn, c2, s2)


def _matres_body(a_ref, w_ref, r_ref, o_ref):
    o_ref[...] = r_ref[...] + _dot(a_ref[...].astype(BF16), w_ref[...].astype(BF16))


def _matres(a, w, res, layer, bm, bn):
    m, k = a.shape
    n = w.shape[-1]
    return pl.pallas_call(
        _matres_body,
        grid=(m // bm, n // bn),
        in_specs=[
            pl.BlockSpec((bm, k), lambda i, j: (i, 0)),
            pl.BlockSpec((None, k, bn), lambda i, j: (layer, 0, j)),
            pl.BlockSpec((bm, bn), lambda i, j: (i, j)),
        ],
        out_specs=pl.BlockSpec((bm, bn), lambda i, j: (i, j)),
        out_shape=jax.ShapeDtypeStruct((m, n), F32),
        compiler_params=_cparams(2),
        name="matres",
    )(a, w, res)


def _retention_tables():
    expo = -5.0 - 7.0 * np.arange(R_HEADS, dtype=np.float64) / max(R_HEADS - 1, 1)
    log_g = np.log1p(-np.exp2(expo))
    idx = np.arange(R_CHUNK, dtype=np.float64)
    diff = idx[:, None] - idx[None, :]
    decay = np.where(diff >= 0, np.exp(log_g[:, None, None] * np.maximum(diff, 0.0)), 0.0)
    row_decay = np.exp(log_g[:, None] * (idx + 1.0))[:, :, None]
    k_decay = np.exp(log_g[:, None] * (R_CHUNK - 1.0 - idx))[:, :, None]
    state_decay = tuple(float(v) for v in np.exp(log_g * R_CHUNK))
    gamma = tuple(float(v) for v in np.exp(log_g))
    return (decay.astype(np.float32), row_decay.astype(np.float32), k_decay.astype(np.float32),
            state_decay, gamma)


def _rope_half(x, cos, sin):
    half = x.shape[-1] // 2
    x1 = x[:, :half]
    x2 = x[:, half:]
    return jnp.concatenate([x1 * cos - x2 * sin, x2 * cos + x1 * sin], axis=-1)


def _layer_norm(x, g, b):
    xc = x - jnp.mean(x, axis=-1, keepdims=True)
    return xc * lax.rsqrt(jnp.mean(xc * xc, axis=-1, keepdims=True) + LN_EPS) * g + b


def _even_prompt_body(p_ref, cos_ref, sin_ref, dec_ref, rd_ref, kd_ref, ws_ref, bs_ref, lng_ref, lnb_ref,
                      o_ref, st_ref, vn_ref, state, *, state_decay):
    c = pl.program_id(1)
    last = pl.num_programs(1) - 1

    @pl.when(c == 0)
    def _zero():
        state[...] = jnp.zeros_like(state)

    cos = cos_ref[...]
    sin = sin_ref[...]
    qk_w = R_HEADS * R_DK
    for h in range(R_HEADS):
        q = _rope_half(p_ref[:, h * R_DK:(h + 1) * R_DK], cos, sin)
        k = _rope_half(p_ref[:, qk_w + h * R_DK:qk_w + (h + 1) * R_DK], cos, sin) * (R_DK ** -0.5)
        v = p_ref[:, 2 * qk_w + h * R_DV:2 * qk_w + (h + 1) * R_DV].astype(BF16)
        g = p_ref[:, 3 * qk_w + h * R_DV:3 * qk_w + (h + 1) * R_DV]
        qb = q.astype(BF16)
        s = _dot_nt(qb, k.astype(BF16)) * dec_ref[h]
        st = state[h]
        o = _dot(s.astype(BF16), v) + _dot(qb, st.astype(BF16)) * rd_ref[h]
        kd_t = (k * kd_ref[h]).T.astype(BF16)
        state[h] = st * state_decay[h] + _dot(kd_t, v)
        o_ref[:, h * R_DV:(h + 1) * R_DV] = (_silu(g) * _rms(o, EPS)).astype(o_ref.dtype)

    u0 = 3 * qk_w + R_HEADS * R_DV
    cmw = CM_GROUPS * CM_GDIM
    vn = _layer_norm(_gelu_tanh(p_ref[:, u0 + cmw:u0 + 2 * cmw]), lng_ref[...], lnb_ref[...])
    row = lax.broadcasted_iota(jnp.int32, (R_CHUNK, R_CHUNK), 0)
    col = lax.broadcasted_iota(jnp.int32, (R_CHUNK, R_CHUNK), 1)
    for gi in range(CM_GROUPS):
        sl = slice(gi * CM_GDIM, (gi + 1) * CM_GDIM)
        w = jnp.where(row >= col, ws_ref[gi], 0.0).astype(BF16)
        mixed = _dot(w, vn[:, sl].astype(BF16)) + bs_ref[gi]
        u = p_ref[:, u0 + gi * CM_GDIM:u0 + (gi + 1) * CM_GDIM]
        o_ref[:, R_HEADS * R_DV + gi * CM_GDIM:R_HEADS * R_DV + (gi + 1) * CM_GDIM] = (
            _gelu_tanh(u) * mixed).astype(o_ref.dtype)

    @pl.when(c == last)
    def _emit():
        st_ref[...] = state[...]
        vn_ref[...] = vn


def _even_prompt(proj, cos, sin, tables, ws, bs4, lng3, lnb3, e, batch, seq):
    decay, row_decay, k_decay, state_decay, _ = tables
    nc = seq // R_CHUNK
    width = proj.shape[-1]
    cmw = CM_GROUPS * CM_GDIM
    out_w = R_HEADS * R_DV + cmw
    const3 = lambda b, c: (0, 0, 0)
    return pl.pallas_call(
        functools.partial(_even_prompt_body, state_decay=state_decay),
        grid=(batch, nc),
        in_specs=[
            pl.BlockSpec((R_CHUNK, width), lambda b, c: (b * nc + c, 0)),
            pl.BlockSpec((R_CHUNK, R_DK // 2), lambda b, c: (c, 0)),
            pl.BlockSpec((R_CHUNK, R_DK // 2), lambda b, c: (c, 0)),
            pl.BlockSpec(decay.shape, const3),
            pl.BlockSpec(row_decay.shape, const3),
            pl.BlockSpec(k_decay.shape, const3),
            pl.BlockSpec((None, CM_GROUPS, R_CHUNK, R_CHUNK), lambda b, c: (e, 0, 0, 0)),
            pl.BlockSpec((None, CM_GROUPS, R_CHUNK, 1), lambda b, c: (e, 0, 0, 0)),
            pl.BlockSpec((None, 1, cmw), lambda b, c: (e, 0, 0)),
            pl.BlockSpec((None, 1, cmw), lambda b, c: (e, 0, 0)),
        ],
        out_specs=[
            pl.BlockSpec((R_CHUNK, out_w), lambda b, c: (b * nc + c, 0)),
            pl.BlockSpec((None, R_HEADS, R_DK, R_DV), lambda b, c: (b, 0, 0, 0)),
            pl.BlockSpec((None, R_CHUNK, cmw), lambda b, c: (b, 0, 0)),
        ],
        out_shape=[
            jax.ShapeDtypeStruct((batch * seq, out_w), BF16),
            jax.ShapeDtypeStruct((batch, R_HEADS, R_DK, R_DV), F32),
            jax.ShapeDtypeStruct((batch, R_CHUNK, cmw), F32),
        ],
        scratch_shapes=[pltpu.VMEM((R_HEADS, R_DK, R_DV), F32)],
        compiler_params=_cparams(2),
        name="even_prompt",
    )(proj, cos, sin, jnp.asarray(decay), jnp.asarray(row_decay), jnp.asarray(k_decay), ws, bs4, lng3, lnb3)


def _even_sample_body(p_ref, cos_ref, sin_ref, st_in_ref, ws_ref, bs_ref, lng_ref, lnb_ref,
                      o_ref, st_ref, vn_ref, *, gamma):
    cos = cos_ref[...]
    sin = sin_ref[...]
    qk_w = R_HEADS * R_DK
    r0 = lax.broadcasted_iota(jnp.int32, (R_DK, R_DK), 0)
    r1 = lax.broadcasted_iota(jnp.int32, (R_DK, R_DK), 1)
    eye = jnp.where(r0 == r1, 1.0, 0.0).astype(BF16)
    for h in range(R_HEADS):
        q = _rope_half(p_ref[:, h * R_DK:(h + 1) * R_DK], cos, sin)
        k = _rope_half(p_ref[:, qk_w + h * R_DK:qk_w + (h + 1) * R_DK], cos, sin) * (R_DK ** -0.5)
        v = p_ref[:, 2 * qk_w + h * R_DV:2 * qk_w + (h + 1) * R_DV]
        g = p_ref[:, 3 * qk_w + h * R_DV:3 * qk_w + (h + 1) * R_DV]
        st = st_in_ref[h]
        s = jnp.sum(q * k, axis=-1, keepdims=True)
        qb = jnp.broadcast_to(q, (8, R_DK)).astype(BF16)
        cross = _dot(qb, st.astype(BF16))[0:1, :]
        o = s * v + cross * gamma[h]
        kb = jnp.broadcast_to(k, (128, R_DK)).astype(BF16)
        kcol = _dot_nt(eye, kb)
        kcol = jnp.concatenate([kcol] * (R_DV // 128), axis=-1)
        vb = v.astype(BF16).astype(F32)
        st_ref[h] = st * gamma[h] + kcol * vb
        o_ref[:, h * R_DV:(h + 1) * R_DV] = (_silu(g) * _rms(o, EPS)).astype(o_ref.dtype)

    u0 = 3 * qk_w + R_HEADS * R_DV
    cmw = CM_GROUPS * CM_GDIM
    vn = _layer_norm(_gelu_tanh(p_ref[:, u0 + cmw:u0 + 2 * cmw]), lng_ref[...], lnb_ref[...])
    vn_ref[...] = vn
    for gi in range(CM_GROUPS):
        sl = slice(gi * CM_GDIM, (gi + 1) * CM_GDIM)
        mixed = ws_ref[gi, 0:1, 0:1] * vn[:, sl] + bs_ref[gi, 0:1, :]
        u = p_ref[:, u0 + gi * CM_GDIM:u0 + (gi + 1) * CM_GDIM]
        o_ref[:, R_HEADS * R_DV + gi * CM_GDIM:R_HEADS * R_DV + (gi + 1) * CM_GDIM] = (
            _gelu_tanh(u) * mixed).astype(o_ref.dtype)


def _even_sample(proj3, cos, sin, tables, st_in, ws, bs4, lng3, lnb3, e):
    gamma = tables[4]
    nb = proj3.shape[0]
    width = proj3.shape[-1]
    cmw = CM_GROUPS * CM_GDIM
    out_w = R_HEADS * R_DV + cmw
    return pl.pallas_call(
        functools.partial(_even_sample_body, gamma=gamma),
        grid=(nb,),
        in_specs=[
            pl.BlockSpec((None, 1, width), lambda b: (b, 0, 0)),
            pl.BlockSpec((1, R_DK // 2), lambda b: (0, 0)),
            pl.BlockSpec((1, R_DK // 2), lambda b: (0, 0)),
            pl.BlockSpec((None, None, R_HEADS, R_DK, R_DV), lambda b: (e, b, 0, 0, 0)),
            pl.BlockSpec((None, CM_GROUPS, R_CHUNK, R_CHUNK), lambda b: (e, 0, 0, 0)),
            pl.BlockSpec((None, CM_GROUPS, R_CHUNK, 1), lambda b: (e, 0, 0, 0)),
            pl.BlockSpec((None, 1, cmw), lambda b: (e, 0, 0)),
            pl.BlockSpec((None, 1, cmw), lambda b: (e, 0, 0)),
        ],
        out_specs=[
            pl.BlockSpec((None, 1, out_w), lambda b: (b, 0, 0)),
            pl.BlockSpec((None, R_HEADS, R_DK, R_DV), lambda b: (b, 0, 0, 0)),
            pl.BlockSpec((None, 1, cmw), lambda b: (b, 0, 0)),
        ],
        out_shape=[
            jax.ShapeDtypeStruct((nb, 1, out_w), F32),
            jax.ShapeDtypeStruct((nb, R_HEADS, R_DK, R_DV), F32),
            jax.ShapeDtypeStruct((nb, 1, cmw), F32),
        ],
        compiler_params=_cparams(1),
        name="even_sample",
    )(proj3, cos, sin, st_in, ws, bs4, lng3, lnb3)


def _diff_lambda(lq1_ref, lk1_ref, lq2_ref, lk2_ref, lam_init):
    a = jnp.sum(lq1_ref[...] * lk1_ref[...], axis=-1, keepdims=True)
    b = jnp.sum(lq2_ref[...] * lk2_ref[...], axis=-1, keepdims=True)
    return jnp.exp(a) - jnp.exp(b) + lam_init


def _flash_body(qi_tab, ki_tab, qt_ref, k_ref, v_ref, lq1_ref, lk1_ref, lq2_ref, lk2_ref, sub_ref,
                o_ref, m_ref, acc_ref, *, scale, lam_init):
    t = pl.program_id(2)
    qi = qi_tab[t]
    ki = ki_tab[t]
    tq = qt_ref.shape[1]
    tk = k_ref.shape[0]
    r = tq // tk
    c = scale * math.log2(math.e)

    @pl.when(ki == 0)
    def _init():
        m_ref[...] = jnp.full_like(m_ref, -jnp.inf)
        acc_ref[...] = jnp.zeros_like(acc_ref)

    def update(modes):
        vt = jnp.concatenate([v_ref[...].T, jnp.ones((BF16_SUBLANES, tk), F32)], axis=0).astype(BF16)
        if "tri" in modes:
            key = lax.broadcasted_iota(jnp.int32, (tk, tk), 0)
            qry = lax.broadcasted_iota(jnp.int32, (tk, tk), 1)
            keep = key <= qry
        for half in range(2):
            sl = slice(half * DA_DH, (half + 1) * DA_DH)
            kb = k_ref[:, sl].astype(BF16)
            for rb, mode in enumerate(modes):
                if mode == "skip":
                    continue
                cols = slice(rb * tk, (rb + 1) * tk)
                st = _dot(kb, qt_ref[sl, cols])
                if mode == "tri":
                    st = jnp.where(keep, st, -jnp.inf)
                m_old = m_ref[half, :, cols]
                m_new = jnp.maximum(m_old, jnp.max(st, axis=0, keepdims=True))
                alpha = jnp.exp2(c * (m_old - m_new))
                pt = jnp.exp2((c * st - c * m_new).astype(BF16))
                acc_ref[half, :, cols] = alpha * acc_ref[half, :, cols] + _dot(vt, pt)
                m_ref[half, :, cols] = m_new

    @pl.when(ki < qi * r)
    def _below_diag():
        update(["full"] * r)

    for kd in range(r):
        @pl.when(ki == qi * r + kd)
        def _on_diag(kd=kd):
            update(["skip"] * kd + ["tri"] + ["full"] * (r - 1 - kd))

    @pl.when(ki == qi * r + (r - 1))
    def _finish():
        lam = _diff_lambda(lq1_ref, lk1_ref, lq2_ref, lk2_ref, lam_init)
        a0 = acc_ref[0]
        a1 = acc_ref[1]
        ot = (a0[0:DA_VDIM] * (1.0 / a0[DA_VDIM:DA_VDIM + 1])
              - lam * (a1[0:DA_VDIM] * (1.0 / a1[DA_VDIM:DA_VDIM + 1])))
        ot = ot * lax.rsqrt(jnp.mean(ot * ot, axis=0, keepdims=True) + SUBLN_EPS)
        o_ref[...] = (ot.T * sub_ref[...] * (1.0 - lam_init)).astype(o_ref.dtype)


def _flash_prompt(q_t, k, v, lam_vecs, subln, batch, seq, lam_init, tq, tk):
    nq = seq // tq
    r = tq // tk
    qi_tab = np.concatenate([np.full((i + 1) * r, i, np.int32) for i in range(nq)])
    ki_tab = np.concatenate([np.arange((i + 1) * r, dtype=np.int32) for i in range(nq)])
    n_tri = int(qi_tab.shape[0])
    vec_spec = pl.BlockSpec((1, DA_DH), lambda b, h, t, qt, kt: (0, 0))
    grid_spec = pltpu.PrefetchScalarGridSpec(
        num_scalar_prefetch=2,
        grid=(batch, DA_HEADS, n_tri),
        in_specs=[
            pl.BlockSpec((DA_VDIM, tq), lambda b, h, t, qt, kt: (h, b * nq + qt[t])),
            pl.BlockSpec((tk, DA_VDIM), lambda b, h, t, qt, kt: (b * nq * r + kt[t], h)),
            pl.BlockSpec((tk, DA_VDIM), lambda b, h, t, qt, kt: (b * nq * r + kt[t], h)),
            vec_spec, vec_spec, vec_spec, vec_spec,
            pl.BlockSpec((1, DA_VDIM), lambda b, h, t, qt, kt: (0, 0)),
        ],
        out_specs=pl.BlockSpec((tq, DA_VDIM), lambda b, h, t, qt, kt: (b * nq + qt[t], h)),
        scratch_shapes=[
            pltpu.VMEM((2, 1, tq), F32),
            pltpu.VMEM((2, DA_VDIM + BF16_SUBLANES, tq), F32),
        ],
    )
    return pl.pallas_call(
        functools.partial(_flash_body, scale=DA_DH ** -0.5, lam_init=lam_init),
        grid_spec=grid_spec,
        out_shape=jax.ShapeDtypeStruct((batch * seq, DA_HEADS * DA_VDIM), BF16),
        compiler_params=_cparams(3),
        name="diff_attn_prompt",
    )(jnp.asarray(qi_tab), jnp.asarray(ki_tab), q_t, k, v, *lam_vecs, subln)


def _decode_body(pt_ref, q_ref, kn_ref, vn_ref, *rest, scale, lam_init, pages_per_step):
    k_refs = rest[:pages_per_step]
    v_refs = rest[pages_per_step:2 * pages_per_step]
    (exp_ref, own_ref, lq1_ref, lk1_ref, lq2_ref, lk2_ref, sub_ref,
     o_ref, m_ref, l_ref, acc_ref) = rest[2 * pages_per_step:]
    step = pl.program_id(1)
    n_hh = 2 * DA_HEADS
    page = k_refs[0].shape[0] // n_hh
    t_len = pages_per_step * page

    @pl.when(step == 0)
    def _init():
        m_ref[...] = jnp.full_like(m_ref, -jnp.inf)
        l_ref[...] = jnp.zeros_like(l_ref)
        acc_ref[...] = jnp.zeros_like(acc_ref)

    qb = q_ref[...].astype(BF16)
    row_t = lax.broadcasted_iota(jnp.int32, (n_hh, t_len), 0)
    s = jnp.zeros((n_hh, t_len), F32)
    for hh in range(n_hh):
        k_hh = jnp.concatenate([kr[pl.ds(hh, page, stride=n_hh), :] for kr in k_refs], axis=0).astype(BF16)
        s = jnp.where(row_t == hh, _dot_nt(qb, k_hh), s)
    s = s * scale
    m_old = m_ref[...]
    m_new = jnp.maximum(m_old, jnp.max(s, axis=-1, keepdims=True))
    alpha = jnp.exp(m_old - m_new)
    p = jnp.exp(s - m_new)
    l_ref[...] = alpha * l_ref[...] + jnp.sum(p, axis=-1, keepdims=True)
    pb = p.astype(BF16)
    pv = jnp.zeros((n_hh, DA_VDIM), F32)
    for r, vr in enumerate(v_refs):
        spread = _dot(pb[:, r * page:(r + 1) * page], exp_ref[...]) * own_ref[...]
        pv += _dot(spread.astype(BF16), vr[...].astype(BF16))
    acc_ref[...] = alpha * acc_ref[...] + pv
    m_ref[...] = m_new

    @pl.when(step == pl.num_programs(1) - 1)
    def _finish():
        s_new = jnp.sum(q_ref[...] * kn_ref[...], axis=-1, keepdims=True) * scale
        m_old2 = m_ref[...]
        m_fin = jnp.maximum(m_old2, s_new)
        alpha2 = jnp.exp(m_old2 - m_fin)
        p_new = jnp.exp(s_new - m_fin)
        l_fin = alpha2 * l_ref[...] + p_new
        a = (alpha2 * acc_ref[...] + p_new * vn_ref[...]) / l_fin
        lam = _diff_lambda(lq1_ref, lk1_ref, lq2_ref, lk2_ref, lam_init)
        for h in range(DA_HEADS):
            o = a[2 * h:2 * h + 1, :] - lam * a[2 * h + 1:2 * h + 2, :]
            o_ref[:, h * DA_VDIM:(h + 1) * DA_VDIM] = (
                _rms(o, SUBLN_EPS) * sub_ref[...] * (1.0 - lam_init)).astype(o_ref.dtype)


def _decode_attn(q3, kn3, vn3, cache_k, cache_v, page_table, lam_vecs, subln, layer, lam_init, pages_per_step):
    nb = q3.shape[0]
    n_hh = 2 * DA_HEADS
    n_layers, n_pool, page = cache_k.shape[:3]
    n_pages = page_table.shape[1]
    n_steps = n_pages // pages_per_step
    cache_k = cache_k.reshape(n_layers, n_pool, page * n_hh, DA_DH)
    cache_v = cache_v.reshape(n_layers, n_pool, page * DA_HEADS, DA_VDIM)

    def page_spec(r, heads, dim):
        return pl.BlockSpec((None, None, page * heads, dim),
                            lambda b, s, pt: (layer, pt[b, s * pages_per_step + r], 0, 0))

    rows_v = np.arange(page * DA_HEADS)
    spread = (rows_v[None, :] // DA_HEADS == np.arange(page)[:, None]).astype(np.float32)
    own = (rows_v[None, :] % DA_HEADS == np.arange(n_hh)[:, None] // 2).astype(np.float32)

    vec_spec = pl.BlockSpec((1, DA_DH), lambda b, s, pt: (0, 0))
    grid_spec = pltpu.PrefetchScalarGridSpec(
        num_scalar_prefetch=1,
        grid=(nb, n_steps),
        in_specs=[pl.BlockSpec((None, n_hh, DA_DH), lambda b, s, pt: (b, 0, 0)),
                  pl.BlockSpec((None, n_hh, DA_DH), lambda b, s, pt: (b, 0, 0)),
                  pl.BlockSpec((None, n_hh, DA_VDIM), lambda b, s, pt: (b, 0, 0))]
        + [page_spec(r, n_hh, DA_DH) for r in range(pages_per_step)]
        + [page_spec(r, DA_HEADS, DA_VDIM) for r in range(pages_per_step)]
        + [pl.BlockSpec(spread.shape, lambda b, s, pt: (0, 0)),
           pl.BlockSpec(own.shape, lambda b, s, pt: (0, 0))]
        + [vec_spec, vec_spec, vec_spec, vec_spec,
           pl.BlockSpec((1, DA_VDIM), lambda b, s, pt: (0, 0))],
        out_specs=pl.BlockSpec((None, 1, DA_HEADS * DA_VDIM), lambda b, s, pt: (b, 0, 0)),
        scratch_shapes=[
            pltpu.VMEM((n_hh, 1), F32),
            pltpu.VMEM((n_hh, 1), F32),
            pltpu.VMEM((n_hh, DA_VDIM), F32),
        ],
    )
    return pl.pallas_call(
        functools.partial(_decode_body, scale=DA_DH ** -0.5, lam_init=lam_init, pages_per_step=pages_per_step),
        grid_spec=grid_spec,
        out_shape=jax.ShapeDtypeStruct((nb, 1, DA_HEADS * DA_VDIM), F32),
        compiler_params=_cparams(2),
        name="diff_attn_decode",
    )(page_table, q3, kn3, vn3, *([cache_k] * pages_per_step), *([cache_v] * pages_per_step),
      jnp.asarray(spread, BF16), jnp.asarray(own), *lam_vecs, subln)


def _rope_tables(pos, d):
    inv = ROPE_THETA ** (-jnp.arange(0, d, 2, dtype=F32) / d)
    ang = pos.astype(F32)[:, None] * inv[None, :]
    return jnp.cos(ang), jnp.sin(ang)


def kernel(x_prompt, x_sample, state_ret, cache_k, cache_v, page_table, ffn_a_norm, ffn_a_w1, ffn_a_w3, ffn_a_w2, mix_norm, ffn_b_norm, ffn_b_w1, ffn_b_w3, ffn_b_w2, ev_w_in, ev_w_out, cm_ws, cm_bs, cm_ln_g, cm_ln_b, od_w_in, od_w_out, da_q_norm, da_k_norm, da_lam_q1, da_lam_k1, da_lam_q2, da_lam_k2, da_subln):
    batch, seq, d = x_prompt.shape
    nb, t_new, _ = x_sample.shape
    depth = ffn_a_norm.shape[0]
    assert t_new == 1 and seq % 1024 == 0 and nb == 8

    bm_p = 1024
    pos_p = jnp.arange(seq, dtype=jnp.int32)
    pos_s = PAST_LEN + jnp.arange(t_new, dtype=jnp.int32)
    cos_e, sin_e = _rope_tables(pos_p, R_DK)
    cos_es, sin_es = _rope_tables(pos_s, R_DK)
    cos_o, sin_o = _rope_tables(pos_p, DA_DH)
    cos_os, sin_os = _rope_tables(pos_s, DA_DH)
    c2_p = jnp.concatenate([cos_o, cos_o], axis=-1)
    s2_p = jnp.concatenate([-sin_o, sin_o], axis=-1)
    c2_s = jnp.broadcast_to(jnp.concatenate([cos_os, cos_os], axis=-1), (nb, DA_DH))
    s2_s = jnp.broadcast_to(jnp.concatenate([-sin_os, sin_os], axis=-1), (nb, DA_DH))
    tables = _retention_tables()

    as3 = lambda a: a.reshape(a.shape[0], 1, a.shape[1])
    ffn_a_norm3, ffn_b_norm3, mix_norm3 = as3(ffn_a_norm), as3(ffn_b_norm), as3(mix_norm)
    lng3, lnb3 = as3(cm_ln_g), as3(cm_ln_b)
    bs4 = cm_bs.reshape(cm_bs.shape + (1,))

    xp = x_prompt.reshape(batch * seq, d)
    xs = x_sample.reshape(nb * t_new, d)
    ret_p, ret_s, cmv_p, cmv_s = [], [], [], []
    kp_l, vp_l, ks_l, vs_l = [], [], [], []

    def ffn_pair(xp, xs, norm3, w1, w3, w2, l):
        return _ffn(xp, xs, norm3, w1, w3, w2, l, bm=bm_p, bf=256)

    for l in range(depth):
        xp, xs = ffn_pair(xp, xs, ffn_a_norm3, ffn_a_w1, ffn_a_w3, ffn_a_w2, l)
        if l % 2 == 0:
            e = l // 2
            even_in = ev_w_in.shape[-1]
            proj_p = _normproj(xp, mix_norm3, l, ev_w_in, e, bm_p, 1024)
            mix_p, st_p, vn_p = _even_prompt(proj_p, cos_e, sin_e, tables, cm_ws, bs4, lng3, lnb3, e, batch, seq)
            proj_s = _normproj(xs, mix_norm3, l, ev_w_in, e, nb, 1024)
            mix_s, st_s, vn_s = _even_sample(proj_s.reshape(nb, 1, even_in), cos_es, sin_es, tables,
                                             state_ret, cm_ws, bs4, lng3, lnb3, e)
            xp = _matres(mix_p, ev_w_out, xp, e, bm_p, 1024)
            xs = _matres(mix_s.reshape(nb, -1), ev_w_out, xs, e, nb, 512)
            ret_p.append(st_p)
            ret_s.append(st_s)
            cmv_p.append(vn_p)
            cmv_s.append(vn_s)
        else:
            o = l // 2
            lam_init = 0.8 - 0.6 * math.exp(-0.3 * l)
            qn = da_q_norm[o].reshape(1, DA_DH)
            kn = da_k_norm[o].reshape(1, DA_DH)
            lam_vecs = [a[o].reshape(1, DA_DH) for a in (da_lam_q1, da_lam_k1, da_lam_q2, da_lam_k2)]
            subln = da_subln[o].reshape(1, DA_VDIM)
            tb = seq // bm_p
            q_p, k_p, v_p = _normproj_qkv(xp, mix_norm3, l, od_w_in, o, bm_p, 512, BF16, qn, kn, c2_p, s2_p, tb)
            att_p = _flash_prompt(q_p.T, k_p, v_p, lam_vecs, subln, batch, seq, lam_init, tq=1024, tk=512)
            q_s, k_s, v_s = _normproj_qkv(xs, mix_norm3, l, od_w_in, o, nb, 512, F32, qn, kn, c2_s, s2_s, 1)
            v_rep = jnp.repeat(v_s.reshape(nb, DA_HEADS, DA_VDIM), 2, axis=1)
            att_s = _decode_attn(q_s.reshape(nb, 2 * DA_HEADS, DA_DH), k_s.reshape(nb, 2 * DA_HEADS, DA_DH), v_rep,
                                 cache_k, cache_v, page_table, lam_vecs, subln, o, lam_init, pages_per_step=8)
            xp = _matres(att_p, od_w_out, xp, o, bm_p, 1024)
            xs = _matres(att_s.reshape(nb, d), od_w_out, xs, o, nb, 512)
            kp_l.append(k_p.reshape(batch, seq, 2 * DA_HEADS, DA_DH))
            vp_l.append(v_p.reshape(batch, seq, DA_HEADS, DA_VDIM))
            ks_l.append(k_s.reshape(nb, t_new, 2 * DA_HEADS, DA_DH))
            vs_l.append(v_s.reshape(nb, t_new, DA_HEADS, DA_VDIM))
        xp, xs = ffn_pair(xp, xs, ffn_b_norm3, ffn_b_w1, ffn_b_w3, ffn_b_w2, l)

    return (xp.reshape(batch, seq, d), xs.reshape(nb, t_new, d),
            jnp.stack(ret_p), jnp.stack(ret_s), jnp.stack(cmv_p), jnp.stack(cmv_s),
            jnp.stack(kp_l), jnp.stack(vp_l), jnp.stack(ks_l), jnp.stack(vs_l))
```

```python
import functools
import math

import numpy as np
import jax
import jax.numpy as jnp
from jax import lax
from jax.experimental import pallas as pl
from jax.experimental.pallas import tpu as pltpu

F32 = jnp.float32
BF16 = jnp.bfloat16

PAST_LEN = 16384
R_HEADS = 4
R_DK = 256
R_DV = 256
R_CHUNK = 128
CM_GROUPS = 4
CM_GDIM = 256
DA_HEADS = 8
DA_DH = 128
DA_VDIM = 256
LOG2_DA_DH = 7
LOG2_DA_VDIM = 8
ROPE_THETA = 10000.0
EPS = 1e-6
SUBLN_EPS = 1e-5
LN_EPS = 1e-5

BF16_SUBLANES = 16
VMEM_LIMIT_BYTES = 56 * 1024 * 1024
FFN_VMEM_LIMIT_BYTES = 60000 * 1024


def _cparams(n_axes, vmem_limit_bytes=VMEM_LIMIT_BYTES):
    return pltpu.CompilerParams(
        dimension_semantics=("arbitrary",) * n_axes,
        vmem_limit_bytes=vmem_limit_bytes,
    )


def _silu(a):
    return a / (1.0 + jnp.exp(-a))


def _gelu_tanh(x):
    c = math.sqrt(2.0 / math.pi)
    return x * (0.5 * (1.0 + jnp.tanh(c * (x + 0.044715 * (x * x * x)))))


def _rms(x, eps):
    return x * lax.rsqrt(jnp.mean(x * x, axis=-1, keepdims=True) + eps)


def _dot(a, b):
    return jnp.dot(a, b, preferred_element_type=F32)


def _dot_nt(a, b):
    return lax.dot_general(a, b, (((1,), (1,)), ((), ())), preferred_element_type=F32)


def _ffn_body(x_ref, xs_ref, g_ref, w1_ref, w3_ref, w2_ref, o_ref, os_ref, xn_ref, *, n_split):
    i = pl.program_id(0)
    j = pl.program_id(1)
    bm = x_ref.shape[0]
    ms = xs_ref.shape[0]
    tail = xn_ref.shape[0] - bm

    @pl.when(j == 0)
    def _init():
        x = x_ref[...]
        xn_ref[0:bm, :] = (_rms(x, EPS) * g_ref[...]).astype(BF16)
        o_ref[...] = x

    @pl.when((i == 0) & (j == 0))
    def _init_sample():
        xs = xs_ref[...]
        xsn = jnp.concatenate([_rms(xs, EPS) * g_ref[...], jnp.zeros((tail - ms, xs.shape[1]), F32)], axis=0)
        xn_ref[bm:bm + tail, :] = xsn.astype(BF16)
        os_ref[...] = xs

    xn = xn_ref[...]
    a = _dot(xn, w1_ref[...].astype(BF16))
    b = _dot(xn, w3_ref[...].astype(BF16))
    h = ((_silu(a) * b) * 0.5).astype(BF16)
    d = o_ref.shape[-1]
    w = d // n_split
    for n in range(n_split):
        sl = slice(n * w, (n + 1) * w)
        y = _dot(h, w2_ref[:, sl].astype(BF16))
        o_ref[:, sl] += y[0:bm]
        os_ref[:, sl] += jnp.where(i == 0, y[bm:bm + ms], 0.0)


def _ffn(x, xs, norm3, w1, w3, w2, layer, bm, bf):
    m, d = x.shape
    ms = xs.shape[0]
    ff = w1.shape[-1]
    return pl.pallas_call(
        functools.partial(_ffn_body, n_split=4),
        grid=(m // bm, ff // bf),
        in_specs=[
            pl.BlockSpec((bm, d), lambda i, j: (i, 0), pipeline_mode=pl.Buffered(1)),
            pl.BlockSpec((ms, d), lambda i, j: (0, 0)),
            pl.BlockSpec((None, 1, d), lambda i, j: (layer, 0, 0)),
            pl.BlockSpec((None, d, bf), lambda i, j: (layer, 0, j)),
            pl.BlockSpec((None, d, bf), lambda i, j: (layer, 0, j)),
            pl.BlockSpec((None, bf, d), lambda i, j: (layer, j, 0)),
        ],
        out_specs=[
            pl.BlockSpec((bm, d), lambda i, j: (i, 0)),
            pl.BlockSpec((ms, d), lambda i, j: (0, 0)),
        ],
        out_shape=[jax.ShapeDtypeStruct((m, d), F32), jax.ShapeDtypeStruct((ms, d), F32)],
        scratch_shapes=[pltpu.VMEM((bm + BF16_SUBLANES, d), BF16)],
        compiler_params=_cparams(2, FFN_VMEM_LIMIT_BYTES),
        name="ffn",
    )(x, xs, norm3, w1, w3, w2)


def _normproj_body(x_ref, g_ref, w_ref, o_ref, xn_ref):
    @pl.when(pl.program_id(1) == 0)
    def _init():
        xn_ref[...] = (_rms(x_ref[...], EPS) * g_ref[...]).astype(BF16)

    o_ref[...] = _dot(xn_ref[...], w_ref[...].astype(BF16)).astype(o_ref.dtype)


def _normproj(x, norm3, norm_layer, w, layer, bm, bn):
    m, d = x.shape
    ncols = w.shape[-1]
    return pl.pallas_call(
        _normproj_body,
        grid=(m // bm, ncols // bn),
        in_specs=[
            pl.BlockSpec((bm, d), lambda i, j: (i, 0)),
            pl.BlockSpec((None, 1, d), lambda i, j: (norm_layer, 0, 0)),
            pl.BlockSpec((None, d, bn), lambda i, j: (layer, 0, j)),
        ],
        out_specs=pl.BlockSpec((bm, bn), lambda i, j: (i, j)),
        out_shape=jax.ShapeDtypeStruct((m, ncols), F32),
        scratch_shapes=[pltpu.VMEM((bm, d), BF16)],
        compiler_params=_cparams(2),
        name="normproj",
    )(x, norm3, w)


def _normproj_qkv_body(x_ref, g_ref, w_ref, qn_ref, kn_ref, c_ref, s_ref, q_ref, k_ref, v_ref, xn_ref, *, nj):
    j = pl.program_id(1)

    @pl.when(j == 0)
    def _init():
        xn_ref[...] = (_rms(x_ref[...], EPS) * g_ref[...]).astype(BF16)

    y = _dot(xn_ref[...], w_ref[...].astype(BF16))

    def norm_rope(hn_ref, o_ref):
        hn = hn_ref[...]
        c = c_ref[...]
        s = s_ref[...]
        for t in range(y.shape[-1] // DA_DH):
            sl = slice(t * DA_DH, (t + 1) * DA_DH)
            yn = _rms(y[:, sl], EPS) * hn
            o_ref[:, sl] = (yn * c + pltpu.roll(yn, DA_DH // 2, axis=1) * s).astype(o_ref.dtype)

    @pl.when(j < nj)
    def _q():
        norm_rope(qn_ref, q_ref)

    @pl.when((j >= nj) & (j < 2 * nj))
    def _k():
        norm_rope(kn_ref, k_ref)

    @pl.when(j >= 2 * nj)
    def _v():
        v_ref[...] = y


def _normproj_qkv(x, norm3, norm_layer, w, layer, bm, bn, q_dtype, qn, kn, c2, s2, table_blocks):
    m, d = x.shape
    nj = d // bn
    clip = lambda j, lo: jnp.clip(j - lo, 0, nj - 1)
    return pl.pallas_call(
        functools.partial(_normproj_qkv_body, nj=nj),
        grid=(m // bm, 3 * nj),
        in_specs=[
            pl.BlockSpec((bm, d), lambda i, j: (i, 0)),
            pl.BlockSpec((None, 1, d), lambda i, j: (norm_layer, 0, 0)),
            pl.BlockSpec((None, d, bn), lambda i, j: (layer, 0, j)),
            pl.BlockSpec((1, DA_DH), lambda i, j: (0, 0)),
            pl.BlockSpec((1, DA_DH), lambda i, j: (0, 0)),
            pl.BlockSpec((bm, DA_DH), lambda i, j: (i % table_blocks, 0)),
            pl.BlockSpec((bm, DA_DH), lambda i, j: (i % table_blocks, 0)),
        ],
        out_specs=[
            pl.BlockSpec((bm, bn), lambda i, j: (i, clip(j, 0))),
            pl.BlockSpec((bm, bn), lambda i, j: (i, clip(j, nj))),
            pl.BlockSpec((bm, bn), lambda i, j: (i, clip(j, 2 * nj))),
        ],
        out_shape=[
            jax.ShapeDtypeStruct((m, d), q_dtype),
            jax.ShapeDtypeStruct((m, d), F32),
            jax.ShapeDtypeStruct((m, d), F32),
        ],
        scratch_shapes=[pltpu.VMEM((bm, d), BF16)],
        compiler_params=_cparams(2),
        name="normproj_qkv",
    )(x, norm3, w, qn, kn, c2, s2)


def _matres_body(a_ref, w_ref, r_ref, o_ref):
    o_ref[...] = r_ref[...] + _dot(a_ref[...].astype(BF16), w_ref[...].astype(BF16))


def _matres(a, w, res, layer, bm, bn):
    m, k = a.shape
    n = w.shape[-1]
    return pl.pallas_call(
        _matres_body,
        grid=(m // bm, n // bn),
        in_specs=[
            pl.BlockSpec((bm, k), lambda i, j: (i, 0)),
            pl.BlockSpec((None, k, bn), lambda i, j: (layer, 0, j)),
            pl.BlockSpec((bm, bn), lambda i, j: (i, j)),
        ],
        out_specs=pl.BlockSpec((bm, bn), lambda i, j: (i, j)),
        out_shape=jax.ShapeDtypeStruct((m, n), F32),
        compiler_params=_cparams(2),
        name="matres",
    )(a, w, res)


def _retention_tables():
    expo = -5.0 - 7.0 * np.arange(R_HEADS, dtype=np.float64) / max(R_HEADS - 1, 1)
    log_g = np.log1p(-np.exp2(expo))
    idx = np.arange(R_CHUNK, dtype=np.float64)
    diff = idx[:, None] - idx[None, :]
    decay = np.where(diff >= 0, np.exp(log_g[:, None, None] * np.maximum(diff, 0.0)), 0.0)
    row_decay = np.exp(log_g[:, None] * (idx + 1.0))[:, :, None]
    k_decay = np.exp(log_g[:, None] * (R_CHUNK - 1.0 - idx))[:, :, None]
    state_decay = tuple(float(v) for v in np.exp(log_g * R_CHUNK))
    gamma = tuple(float(v) for v in np.exp(log_g))
    return (decay.astype(np.float32), row_decay.astype(np.float32), k_decay.astype(np.float32),
            state_decay, gamma)


def _rope_half(x, cos, sin):
    half = x.shape[-1] // 2
    x1 = x[:, :half]
    x2 = x[:, half:]
    return jnp.concatenate([x1 * cos - x2 * sin, x2 * cos + x1 * sin], axis=-1)


def _layer_norm(x, g, b):
    xc = x - jnp.mean(x, axis=-1, keepdims=True)
    return xc * lax.rsqrt(jnp.mean(xc * xc, axis=-1, keepdims=True) + LN_EPS) * g + b


def _even_prompt_body(p_ref, cos_ref, sin_ref, dec_ref, rd_ref, kd_ref, ws_ref, bs_ref, lng_ref, lnb_ref,
                      o_ref, st_ref, vn_ref, state, *, state_decay):
    c = pl.program_id(1)
    last = pl.num_programs(1) - 1

    @pl.when(c == 0)
    def _zero():
        state[...] = jnp.zeros_like(state)

    cos = cos_ref[...]
    sin = sin_ref[...]
    qk_w = R_HEADS * R_DK
    for h in range(R_HEADS):
        q = _rope_half(p_ref[:, h * R_DK:(h + 1) * R_DK], cos, sin)
        k = _rope_half(p_ref[:, qk_w + h * R_DK:qk_w + (h + 1) * R_DK], cos, sin) * (R_DK ** -0.5)
        v = p_ref[:, 2 * qk_w + h * R_DV:2 * qk_w + (h + 1) * R_DV].astype(BF16)
        g = p_ref[:, 3 * qk_w + h * R_DV:3 * qk_w + (h + 1) * R_DV]
        qb = q.astype(BF16)
        s = _dot_nt(qb, k.astype(BF16)) * dec_ref[h]
        st = state[h]
        o = _dot(s.astype(BF16), v) + _dot(qb, st.astype(BF16)) * rd_ref[h]
        kd_t = (k * kd_ref[h]).T.astype(BF16)
        state[h] = st * state_decay[h] + _dot(kd_t, v)
        o_ref[:, h * R_DV:(h + 1) * R_DV] = (_silu(g) * _rms(o, EPS)).astype(o_ref.dtype)

    u0 = 3 * qk_w + R_HEADS * R_DV
    cmw = CM_GROUPS * CM_GDIM
    vn = _layer_norm(_gelu_tanh(p_ref[:, u0 + cmw:u0 + 2 * cmw]), lng_ref[...], lnb_ref[...])
    row = lax.broadcasted_iota(jnp.int32, (R_CHUNK, R_CHUNK), 0)
    col = lax.broadcasted_iota(jnp.int32, (R_CHUNK, R_CHUNK), 1)
    for gi in range(CM_GROUPS):
        sl = slice(gi * CM_GDIM, (gi + 1) * CM_GDIM)
        w = jnp.where(row >= col, ws_ref[gi], 0.0).astype(BF16)
        mixed = _dot(w, vn[:, sl].astype(BF16)) + bs_ref[gi]
        u = p_ref[:, u0 + gi * CM_GDIM:u0 + (gi + 1) * CM_GDIM]
        o_ref[:, R_HEADS * R_DV + gi * CM_GDIM:R_HEADS * R_DV + (gi + 1) * CM_GDIM] = (
            _gelu_tanh(u) * mixed).astype(o_ref.dtype)

    @pl.when(c == last)
    def _emit():
        st_ref[...] = state[...]
        vn_ref[...] = vn


def _even_prompt(proj, cos, sin, tables, ws, bs4, lng3, lnb3, e, batch, seq):
    decay, row_decay, k_decay, state_decay, _ = tables
    nc = seq // R_CHUNK
    width = proj.shape[-1]
    cmw = CM_GROUPS * CM_GDIM
    out_w = R_HEADS * R_DV + cmw
    const3 = lambda b, c: (0, 0, 0)
    return pl.pallas_call(
        functools.partial(_even_prompt_body, state_decay=state_decay),
        grid=(batch, nc),
        in_specs=[
            pl.BlockSpec((R_CHUNK, width), lambda b, c: (b * nc + c, 0)),
            pl.BlockSpec((R_CHUNK, R_DK // 2), lambda b, c: (c, 0)),
            pl.BlockSpec((R_CHUNK, R_DK // 2), lambda b, c: (c, 0)),
            pl.BlockSpec(decay.shape, const3),
            pl.BlockSpec(row_decay.shape, const3),
            pl.BlockSpec(k_decay.shape, const3),
            pl.BlockSpec((None, CM_GROUPS, R_CHUNK, R_CHUNK), lambda b, c: (e, 0, 0, 0)),
            pl.BlockSpec((None, CM_GROUPS, R_CHUNK, 1), lambda b, c: (e, 0, 0, 0)),
            pl.BlockSpec((None, 1, cmw), lambda b, c: (e, 0, 0)),
            pl.BlockSpec((None, 1, cmw), lambda b, c: (e, 0, 0)),
        ],
        out_specs=[
            pl.BlockSpec((R_CHUNK, out_w), lambda b, c: (b * nc + c, 0)),
            pl.BlockSpec((None, R_HEADS, R_DK, R_DV), lambda b, c: (b, 0, 0, 0)),
            pl.BlockSpec((None, R_CHUNK, cmw), lambda b, c: (b, 0, 0)),
        ],
        out_shape=[
            jax.ShapeDtypeStruct((batch * seq, out_w), BF16),
            jax.ShapeDtypeStruct((batch, R_HEADS, R_DK, R_DV), F32),
            jax.ShapeDtypeStruct((batch, R_CHUNK, cmw), F32),
        ],
        scratch_shapes=[pltpu.VMEM((R_HEADS, R_DK, R_DV), F32)],
        compiler_params=_cparams(2),
        name="even_prompt",
    )(proj, cos, sin, jnp.asarray(decay), jnp.asarray(row_decay), jnp.asarray(k_decay), ws, bs4, lng3, lnb3)


def _even_sample_body(p_ref, cos_ref, sin_ref, st_in_ref, ws_ref, bs_ref, lng_ref, lnb_ref,
                      o_ref, st_ref, vn_ref, *, gamma):
    cos = cos_ref[...]
    sin = sin_ref[...]
    qk_w = R_HEADS * R_DK
    r0 = lax.broadcasted_iota(jnp.int32, (R_DK, R_DK), 0)
    r1 = lax.broadcasted_iota(jnp.int32, (R_DK, R_DK), 1)
    eye = jnp.where(r0 == r1, 1.0, 0.0).astype(BF16)
    for h in range(R_HEADS):
        q = _rope_half(p_ref[:, h * R_DK:(h + 1) * R_DK], cos, sin)
        k = _rope_half(p_ref[:, qk_w + h * R_DK:qk_w + (h + 1) * R_DK], cos, sin) * (R_DK ** -0.5)
        v = p_ref[:, 2 * qk_w + h * R_DV:2 * qk_w + (h + 1) * R_DV]
        g = p_ref[:, 3 * qk_w + h * R_DV:3 * qk_w + (h + 1) * R_DV]
        st = st_in_ref[h]
        s = jnp.sum(q * k, axis=-1, keepdims=True)
        qb = jnp.broadcast_to(q, (8, R_DK)).astype(BF16)
        cross = _dot(qb, st.astype(BF16))[0:1, :]
        o = s * v + cross * gamma[h]
        kb = jnp.broadcast_to(k, (128, R_DK)).astype(BF16)
        kcol = _dot_nt(eye, kb)
        kcol = jnp.concatenate([kcol] * (R_DV // 128), axis=-1)
        vb = v.astype(BF16).astype(F32)
        st_ref[h] = st * gamma[h] + kcol * vb
        o_ref[:, h * R_DV:(h + 1) * R_DV] = (_silu(g) * _rms(o, EPS)).astype(o_ref.dtype)

    u0 = 3 * qk_w + R_HEADS * R_DV
    cmw = CM_GROUPS * CM_GDIM
    vn = _layer_norm(_gelu_tanh(p_ref[:, u0 + cmw:u0 + 2 * cmw]), lng_ref[...], lnb_ref[...])
    vn_ref[...] = vn
    for gi in range(CM_GROUPS):
        sl = slice(gi * CM_GDIM, (gi + 1) * CM_GDIM)
        mixed = ws_ref[gi, 0:1, 0:1] * vn[:, sl] + bs_ref[gi, 0:1, :]
        u = p_ref[:, u0 + gi * CM_GDIM:u0 + (gi + 1) * CM_GDIM]
        o_ref[:, R_HEADS * R_DV + gi * CM_GDIM:R_HEADS * R_DV + (gi + 1) * CM_GDIM] = (
            _gelu_tanh(u) * mixed).astype(o_ref.dtype)


def _even_sample(proj3, cos, sin, tables, st_in, ws, bs4, lng3, lnb3, e):
    gamma = tables[4]
    nb = proj3.shape[0]
    width = proj3.shape[-1]
    cmw = CM_GROUPS * CM_GDIM
    out_w = R_HEADS * R_DV + cmw
    return pl.pallas_call(
        functools.partial(_even_sample_body, gamma=gamma),
        grid=(nb,),
        in_specs=[
            pl.BlockSpec((None, 1, width), lambda b: (b, 0, 0)),
            pl.BlockSpec((1, R_DK // 2), lambda b: (0, 0)),
            pl.BlockSpec((1, R_DK // 2), lambda b: (0, 0)),
            pl.BlockSpec((None, None, R_HEADS, R_DK, R_DV), lambda b: (e, b, 0, 0, 0)),
            pl.BlockSpec((None, CM_GROUPS, R_CHUNK, R_CHUNK), lambda b: (e, 0, 0, 0)),
            pl.BlockSpec((None, CM_GROUPS, R_CHUNK, 1), lambda b: (e, 0, 0, 0)),
            pl.BlockSpec((None, 1, cmw), lambda b: (e, 0, 0)),
            pl.BlockSpec((None, 1, cmw), lambda b: (e, 0, 0)),
        ],
        out_specs=[
            pl.BlockSpec((None, 1, out_w), lambda b: (b, 0, 0)),
            pl.BlockSpec((None, R_HEADS, R_DK, R_DV), lambda b: (b, 0, 0, 0)),
            pl.BlockSpec((None, 1, cmw), lambda b: (b, 0, 0)),
        ],
        out_shape=[
            jax.ShapeDtypeStruct((nb, 1, out_w), F32),
            jax.ShapeDtypeStruct((nb, R_HEADS, R_DK, R_DV), F32),
            jax.ShapeDtypeStruct((nb, 1, cmw), F32),
        ],
        compiler_params=_cparams(1),
        name="even_sample",
    )(proj3, cos, sin, st_in, ws, bs4, lng3, lnb3)


def _diff_lambda(lq1_ref, lk1_ref, lq2_ref, lk2_ref, lam_init):
    a = jnp.sum(lq1_ref[...] * lk1_ref[...], axis=-1, keepdims=True)
    b = jnp.sum(lq2_ref[...] * lk2_ref[...], axis=-1, keepdims=True)
    return jnp.exp(a) - jnp.exp(b) + lam_init


def _flash_body(qi_tab, ki_tab, qt_ref, k_ref, v_ref, lq1_ref, lk1_ref, lq2_ref, lk2_ref, sub_ref,
                o_ref, m_ref, acc_ref, *, scale, lam_init):
    t = pl.program_id(2)
    qi = qi_tab[t]
    ki = ki_tab[t]
    tq = qt_ref.shape[1]
    tk = k_ref.shape[0]
    r = tq // tk
    c = scale * math.log2(math.e)

    @pl.when(ki == 0)
    def _init():
        m_ref[...] = jnp.full_like(m_ref, -jnp.inf)
        acc_ref[...] = jnp.zeros_like(acc_ref)

    def update(modes):
        vt = jnp.concatenate([v_ref[...].T, jnp.ones((BF16_SUBLANES, tk), F32)], axis=0).astype(BF16)
        if "tri" in modes:
            key = lax.broadcasted_iota(jnp.int32, (tk, tk), 0)
            qry = lax.broadcasted_iota(jnp.int32, (tk, tk), 1)
            keep = key <= qry
        for half in range(2):
            sl = slice(half * DA_DH, (half + 1) * DA_DH)
            kb = k_ref[:, sl].astype(BF16)
            for rb, mode in enumerate(modes):
                if mode == "skip":
                    continue
                cols = slice(rb * tk, (rb + 1) * tk)
                st = _dot(kb, qt_ref[sl, cols])
                if mode == "tri":
                    st = jnp.where(keep, st, -jnp.inf)
                m_old = m_ref[half, :, cols]
                m_new = jnp.maximum(m_old, jnp.max(st, axis=0, keepdims=True))
                alpha = jnp.exp2(c * (m_old - m_new))
                pt = jnp.exp2((c * st - c * m_new).astype(BF16))
                acc_ref[half, :, cols] = alpha * acc_ref[half, :, cols] + _dot(vt, pt)
                m_ref[half, :, cols] = m_new

    @pl.when(ki < qi * r)
    def _below_diag():
        update(["full"] * r)

    for kd in range(r):
        @pl.when(ki == qi * r + kd)
        def _on_diag(kd=kd):
            update(["skip"] * kd + ["tri"] + ["full"] * (r - 1 - kd))

    @pl.when(ki == qi * r + (r - 1))
    def _finish():
        lam = _diff_lambda(lq1_ref, lk1_ref, lq2_ref, lk2_ref, lam_init)
        a0 = acc_ref[0]
        a1 = acc_ref[1]
        ot = (a0[0:DA_VDIM] * (1.0 / a0[DA_VDIM:DA_VDIM + 1])
              - lam * (a1[0:DA_VDIM] * (1.0 / a1[DA_VDIM:DA_VDIM + 1])))
        ot = ot * lax.rsqrt(jnp.mean(ot * ot, axis=0, keepdims=True) + SUBLN_EPS)
        o_ref[...] = (ot.T * sub_ref[...] * (1.0 - lam_init)).astype(o_ref.dtype)


def _flash_prompt(q_t, k, v, lam_vecs, subln, batch, seq, lam_init, tq, tk):
    nq = seq // tq
    r = tq // tk
    qi_tab = np.concatenate([np.full((i + 1) * r, i, np.int32) for i in range(nq)])
    ki_tab = np.concatenate([np.arange((i + 1) * r, dtype=np.int32) for i in range(nq)])
    n_tri = int(qi_tab.shape[0])
    vec_spec = pl.BlockSpec((1, DA_DH), lambda b, h, t, qt, kt: (0, 0))
    grid_spec = pltpu.PrefetchScalarGridSpec(
        num_scalar_prefetch=2,
        grid=(batch, DA_HEADS, n_tri),
        in_specs=[
            pl.BlockSpec((DA_VDIM, tq), lambda b, h, t, qt, kt: (h, b * nq + qt[t])),
            pl.BlockSpec((tk, DA_VDIM), lambda b, h, t, qt, kt: (b * nq * r + kt[t], h)),
            pl.BlockSpec((tk, DA_VDIM), lambda b, h, t, qt, kt: (b * nq * r + kt[t], h)),
            vec_spec, vec_spec, vec_spec, vec_spec,
            pl.BlockSpec((1, DA_VDIM), lambda b, h, t, qt, kt: (0, 0)),
        ],
        out_specs=pl.BlockSpec((tq, DA_VDIM), lambda b, h, t, qt, kt: (b * nq + qt[t], h)),
        scratch_shapes=[
            pltpu.VMEM((2, 1, tq), F32),
            pltpu.VMEM((2, DA_VDIM + BF16_SUBLANES, tq), F32),
        ],
    )
    return pl.pallas_call(
        functools.partial(_flash_body, scale=DA_DH ** -0.5, lam_init=lam_init),
        grid_spec=grid_spec,
        out_shape=jax.ShapeDtypeStruct((batch * seq, DA_HEADS * DA_VDIM), BF16),
        compiler_params=_cparams(3),
        name="diff_attn_prompt",
    )(jnp.asarray(qi_tab), jnp.asarray(ki_tab), q_t, k, v, *lam_vecs, subln)


def _decode_body(pt_ref, q_ref, kn_ref, vn_ref, *rest, scale, lam_init, pages_per_step):
    k_refs = rest[:pages_per_step]
    v_refs = rest[pages_per_step:2 * pages_per_step]
    (own_ref, lq1_ref, lk1_ref, lq2_ref, lk2_ref, sub_ref,
     o_ref, m_ref, l_ref, acc_ref) = rest[2 * pages_per_step:]
    step = pl.program_id(1)
    n_hh = 2 * DA_HEADS
    rows_v = v_refs[0].shape[0]

    @pl.when(step == 0)
    def _init():
        m_ref[...] = jnp.full_like(m_ref, -jnp.inf)
        l_ref[...] = jnp.zeros_like(l_ref)
        acc_ref[...] = jnp.zeros_like(acc_ref)

    q = q_ref[...]
    parity = lax.broadcasted_iota(jnp.int32, (n_hh, DA_DH), 0) & 1
    q2 = jnp.concatenate([jnp.where(parity == 0, q, 0.0), jnp.where(parity == 1, q, 0.0)], axis=1).astype(BF16)
    own = own_ref[...] > 0.0
    s_parts = []
    for kr in k_refs:
        k2 = jnp.concatenate([kr[pl.ds(0, rows_v, stride=2), :], kr[pl.ds(1, rows_v, stride=2), :]],
                             axis=1).astype(BF16)
        s_parts.append(jnp.where(own, _dot_nt(q2, k2), -jnp.inf))
    s = jnp.concatenate(s_parts, axis=1) * scale
    m_old = m_ref[...]
    m_new = jnp.maximum(m_old, jnp.max(s, axis=-1, keepdims=True))
    alpha = jnp.exp(m_old - m_new)
    p = jnp.exp(s - m_new)
    l_ref[...] = alpha * l_ref[...] + jnp.sum(p, axis=-1, keepdims=True)
    pb = p.astype(BF16)
    pv = jnp.zeros((n_hh, DA_VDIM), F32)
    for r, vr in enumerate(v_refs):
        pv += _dot(pb[:, r * rows_v:(r + 1) * rows_v], vr[...].astype(BF16))
    acc_ref[...] = alpha * acc_ref[...] + pv
    m_ref[...] = m_new

    @pl.when(step == pl.num_programs(1) - 1)
    def _finish():
        s_new = jnp.sum(q_ref[...] * kn_ref[...], axis=-1, keepdims=True) * scale
        m_old2 = m_ref[...]
        m_fin = jnp.maximum(m_old2, s_new)
        alpha2 = jnp.exp(m_old2 - m_fin)
        p_new = jnp.exp(s_new - m_fin)
        l_fin = alpha2 * l_ref[...] + p_new
        a = (alpha2 * acc_ref[...] + p_new * vn_ref[...]) / l_fin
        lam = _diff_lambda(lq1_ref, lk1_ref, lq2_ref, lk2_ref, lam_init)
        for h in range(DA_HEADS):
            o = a[2 * h:2 * h + 1, :] - lam * a[2 * h + 1:2 * h + 2, :]
            o_ref[:, h * DA_VDIM:(h + 1) * DA_VDIM] = (
                _rms(o, SUBLN_EPS) * sub_ref[...] * (1.0 - lam_init)).astype(o_ref.dtype)


def _decode_attn(q3, kn3, vn3, cache_k, cache_v, page_table, lam_vecs, subln, layer, lam_init, pages_per_step):
    nb = q3.shape[0]
    n_hh = 2 * DA_HEADS
    n_layers, n_pool, page = cache_k.shape[:3]
    n_pages = page_table.shape[1]
    n_steps = n_pages // pages_per_step
    cache_k = cache_k.reshape(n_layers, n_pool, page * n_hh, DA_DH)
    cache_v = cache_v.reshape(n_layers, n_pool, page * DA_HEADS, DA_VDIM)

    def page_spec(r, heads, dim):
        return pl.BlockSpec((None, None, page * heads, dim),
                            lambda b, s, pt: (layer, pt[b, s * pages_per_step + r], 0, 0))

    rows_v = np.arange(page * DA_HEADS)
    own = (rows_v[None, :] % DA_HEADS == np.arange(n_hh)[:, None] // 2).astype(np.float32)

    vec_spec = pl.BlockSpec((1, DA_DH), lambda b, s, pt: (0, 0))
    grid_spec = pltpu.PrefetchScalarGridSpec(
        num_scalar_prefetch=1,
        grid=(nb, n_steps),
        in_specs=[pl.BlockSpec((None, n_hh, DA_DH), lambda b, s, pt: (b, 0, 0)),
                  pl.BlockSpec((None, n_hh, DA_DH), lambda b, s, pt: (b, 0, 0)),
                  pl.BlockSpec((None, n_hh, DA_VDIM), lambda b, s, pt: (b, 0, 0))]
        + [page_spec(r, n_hh, DA_DH) for r in range(pages_per_step)]
        + [page_spec(r, DA_HEADS, DA_VDIM) for r in range(pages_per_step)]
        + [pl.BlockSpec(own.shape, lambda b, s, pt: (0, 0))]
        + [vec_spec, vec_spec, vec_spec, vec_spec,
           pl.BlockSpec((1, DA_VDIM), lambda b, s, pt: (0, 0))],
        out_specs=pl.BlockSpec((None, 1, DA_HEADS * DA_VDIM), lambda b, s, pt: (b, 0, 0)),
        scratch_shapes=[
            pltpu.VMEM((n_hh, 1), F32),
            pltpu.VMEM((n_hh, 1), F32),
            pltpu.VMEM((n_hh, DA_VDIM), F32),
        ],
    )
    return pl.pallas_call(
        functools.partial(_decode_body, scale=DA_DH ** -0.5, lam_init=lam_init, pages_per_step=pages_per_step),
        grid_spec=grid_spec,
        out_shape=jax.ShapeDtypeStruct((nb, 1, DA_HEADS * DA_VDIM), F32),
        compiler_params=_cparams(2),
        name="diff_attn_decode",
    )(page_table, q3, kn3, vn3, *([cache_k] * pages_per_step), *([cache_v] * pages_per_step),
      jnp.asarray(own), *lam_vecs, subln)


def _rope_tables(pos, d):
    inv = ROPE_THETA ** (-jnp.arange(0, d, 2, dtype=F32) / d)
    ang = pos.astype(F32)[:, None] * inv[None, :]
    return jnp.cos(ang), jnp.sin(ang)


def kernel(x_prompt, x_sample, state_ret, cache_k, cache_v, page_table, ffn_a_norm, ffn_a_w1, ffn_a_w3, ffn_a_w2, mix_norm, ffn_b_norm, ffn_b_w1, ffn_b_w3, ffn_b_w2, ev_w_in, ev_w_out, cm_ws, cm_bs, cm_ln_g, cm_ln_b, od_w_in, od_w_out, da_q_norm, da_k_norm, da_lam_q1, da_lam_k1, da_lam_q2, da_lam_k2, da_subln):
    batch, seq, d = x_prompt.shape
    nb, t_new, _ = x_sample.shape
    depth = ffn_a_norm.shape[0]
    assert t_new == 1 and seq % 1024 == 0 and nb == 8

    bm_p = 1024
    pos_p = jnp.arange(seq, dtype=jnp.int32)
    pos_s = PAST_LEN + jnp.arange(t_new, dtype=jnp.int32)
    cos_e, sin_e = _rope_tables(pos_p, R_DK)
    cos_es, sin_es = _rope_tables(pos_s, R_DK)
    cos_o, sin_o = _rope_tables(pos_p, DA_DH)
    cos_os, sin_os = _rope_tables(pos_s, DA_DH)
    c2_p = jnp.concatenate([cos_o, cos_o], axis=-1)
    s2_p = jnp.concatenate([-sin_o, sin_o], axis=-1)
    c2_s = jnp.broadcast_to(jnp.concatenate([cos_os, cos_os], axis=-1), (nb, DA_DH))
    s2_s = jnp.broadcast_to(jnp.concatenate([-sin_os, sin_os], axis=-1), (nb, DA_DH))
    tables = _retention_tables()

    as3 = lambda a: a.reshape(a.shape[0], 1, a.shape[1])
    ffn_a_norm3, ffn_b_norm3, mix_norm3 = as3(ffn_a_norm), as3(ffn_b_norm), as3(mix_norm)
    lng3, lnb3 = as3(cm_ln_g), as3(cm_ln_b)
    bs4 = cm_bs.reshape(cm_bs.shape + (1,))

    xp = x_prompt.reshape(batch * seq, d)
    xs = x_sample.reshape(nb * t_new, d)
    ret_p, ret_s, cmv_p, cmv_s = [], [], [], []
    kp_l, vp_l, ks_l, vs_l = [], [], [], []

    def ffn_pair(xp, xs, norm3, w1, w3, w2, l):
        return _ffn(xp, xs, norm3, w1, w3, w2, l, bm=bm_p, bf=512)

    for l in range(depth):
        xp, xs = ffn_pair(xp, xs, ffn_a_norm3, ffn_a_w1, ffn_a_w3, ffn_a_w2, l)
        if l % 2 == 0:
            e = l // 2
            even_in = ev_w_in.shape[-1]
            proj_p = _normproj(xp, mix_norm3, l, ev_w_in, e, bm_p, 1024)
            mix_p, st_p, vn_p = _even_prompt(proj_p, cos_e, sin_e, tables, cm_ws, bs4, lng3, lnb3, e, batch, seq)
            proj_s = _normproj(xs, mix_norm3, l, ev_w_in, e, nb, 1024)
            mix_s, st_s, vn_s = _even_sample(proj_s.reshape(nb, 1, even_in), cos_es, sin_es, tables,
                                             state_ret, cm_ws, bs4, lng3, lnb3, e)
            xp = _matres(mix_p, ev_w_out, xp, e, bm_p, 1024)
            xs = _matres(mix_s.reshape(nb, -1), ev_w_out, xs, e, nb, 512)
            ret_p.append(st_p)
            ret_s.append(st_s)
            cmv_p.append(vn_p)
            cmv_s.append(vn_s)
        else:
            o = l // 2
            lam_init = 0.8 - 0.6 * math.exp(-0.3 * l)
            qn = da_q_norm[o].reshape(1, DA_DH)
            kn = da_k_norm[o].reshape(1, DA_DH)
            lam_vecs = [a[o].reshape(1, DA_DH) for a in (da_lam_q1, da_lam_k1, da_lam_q2, da_lam_k2)]
            subln = da_subln[o].reshape(1, DA_VDIM)
            tb = seq // bm_p
            q_p, k_p, v_p = _normproj_qkv(xp, mix_norm3, l, od_w_in, o, bm_p, 512, BF16, qn, kn, c2_p, s2_p, tb)
            att_p = _flash_prompt(q_p.T, k_p, v_p, lam_vecs, subln, batch, seq, lam_init, tq=1024, tk=512)
            q_s, k_s, v_s = _normproj_qkv(xs, mix_norm3, l, od_w_in, o, nb, 512, F32, qn, kn, c2_s, s2_s, 1)
            v_rep = jnp.repeat(v_s.reshape(nb, DA_HEADS, DA_VDIM), 2, axis=1)
            att_s = _decode_attn(q_s.reshape(nb, 2 * DA_HEADS, DA_DH), k_s.reshape(nb, 2 * DA_HEADS, DA_DH), v_rep,
                                 cache_k, cache_v, page_table, lam_vecs, subln, o, lam_init, pages_per_step=8)
            xp = _matres(att_p, od_w_out, xp, o, bm_p, 1024)
            xs = _matres(att_s.reshape(nb, d), od_w_out, xs, o, nb, 512)
            kp_l.append(k_p.reshape(batch, seq, 2 * DA_HEADS, DA_DH))
            vp_l.append(v_p.reshape(batch, seq, DA_HEADS, DA_VDIM))
            ks_l.append(k_s.reshape(nb, t_new, 2 * DA_HEADS, DA_DH))
            vs_l.append(v_s.reshape(nb, t_new, DA_HEADS, DA_VDIM))
        xp, xs = ffn_pair(xp, xs, ffn_b_norm3, ffn_b_w1, ffn_b_w3, ffn_b_w2, l)

    return (xp.reshape(batch, seq, d), xs.reshape(nb, t_new, d),
            jnp.stack(ret_p), jnp.stack(ret_s), jnp.stack(cmv_p), jnp.stack(cmv_s),
            jnp.stack(kp_l), jnp.stack(vp_l), jnp.stack(ks_l), jnp.stack(vs_l))
```

```python
import functools
import math

import numpy as np
import jax
import jax.numpy as jnp
from jax import lax
from jax.experimental import pallas as pl
from jax.experimental.pallas import tpu as pltpu

F32 = jnp.float32
BF16 = jnp.bfloat16

PAST_LEN = 16384
R_HEADS = 4
R_DK = 256
R_DV = 256
R_CHUNK = 128
CM_GROUPS = 4
CM_GDIM = 256
DA_HEADS = 8
DA_DH = 128
DA_VDIM = 256
LOG2_DA_DH = 7
LOG2_DA_VDIM = 8
ROPE_THETA = 10000.0
EPS = 1e-6
SUBLN_EPS = 1e-5
LN_EPS = 1e-5

BF16_SUBLANES = 16
VMEM_LIMIT_BYTES = 56 * 1024 * 1024
FFN_VMEM_LIMIT_BYTES = 60000 * 1024


def _cparams(n_axes, vmem_limit_bytes=VMEM_LIMIT_BYTES):
    return pltpu.CompilerParams(
        dimension_semantics=("arbitrary",) * n_axes,
        vmem_limit_bytes=vmem_limit_bytes,
    )


def _silu(a):
    return a / (1.0 + jnp.exp(-a))


def _gelu_tanh(x):
    c = math.sqrt(2.0 / math.pi)
    return x * (0.5 * (1.0 + jnp.tanh(c * (x + 0.044715 * (x * x * x)))))


def _rms(x, eps):
    return x * lax.rsqrt(jnp.mean(x * x, axis=-1, keepdims=True) + eps)


def _dot(a, b):
    return jnp.dot(a, b, preferred_element_type=F32)


def _dot_nt(a, b):
    return lax.dot_general(a, b, (((1,), (1,)), ((), ())), preferred_element_type=F32)


def _ffn_body(x_ref, xs_ref, g_ref, w1_ref, w3_ref, w2_ref, o_ref, os_ref, xn_ref, *, n_split):
    i = pl.program_id(0)
    j = pl.program_id(1)
    bm = x_ref.shape[0]
    ms = xs_ref.shape[0]
    tail = xn_ref.shape[0] - bm

    @pl.when(j == 0)
    def _init():
        x = x_ref[...]
        xn_ref[0:bm, :] = (_rms(x, EPS) * g_ref[...]).astype(BF16)
        o_ref[...] = x

    @pl.when((i == 0) & (j == 0))
    def _init_sample():
        xs = xs_ref[...]
        xsn = jnp.concatenate([_rms(xs, EPS) * g_ref[...], jnp.zeros((tail - ms, xs.shape[1]), F32)], axis=0)
        xn_ref[bm:bm + tail, :] = xsn.astype(BF16)
        os_ref[...] = xs

    xn = xn_ref[...]
    a = _dot(xn, w1_ref[...].astype(BF16))
    b = _dot(xn, w3_ref[...].astype(BF16))
    h = ((_silu(a) * b) * 0.5).astype(BF16)
    d = o_ref.shape[-1]
    w = d // n_split
    for n in range(n_split):
        sl = slice(n * w, (n + 1) * w)
        y = _dot(h, w2_ref[:, sl].astype(BF16))
        o_ref[:, sl] += y[0:bm]
        os_ref[:, sl] += jnp.where(i == 0, y[bm:bm + ms], 0.0)


def _ffn(x, xs, norm3, w1, w3, w2, layer, bm, bf):
    m, d = x.shape
    ms = xs.shape[0]
    ff = w1.shape[-1]
    return pl.pallas_call(
        functools.partial(_ffn_body, n_split=4),
        grid=(m // bm, ff // bf),
        in_specs=[
            pl.BlockSpec((bm, d), lambda i, j: (i, 0), pipeline_mode=pl.Buffered(1)),
            pl.BlockSpec((ms, d), lambda i, j: (0, 0)),
            pl.BlockSpec((None, 1, d), lambda i, j: (layer, 0, 0)),
            pl.BlockSpec((None, d, bf), lambda i, j: (layer, 0, j)),
            pl.BlockSpec((None, d, bf), lambda i, j: (layer, 0, j)),
            pl.BlockSpec((None, bf, d), lambda i, j: (layer, j, 0)),
        ],
        out_specs=[
            pl.BlockSpec((bm, d), lambda i, j: (i, 0)),
            pl.BlockSpec((ms, d), lambda i, j: (0, 0)),
        ],
        out_shape=[jax.ShapeDtypeStruct((m, d), F32), jax.ShapeDtypeStruct((ms, d), F32)],
        scratch_shapes=[pltpu.VMEM((bm + BF16_SUBLANES, d), BF16)],
        compiler_params=_cparams(2, FFN_VMEM_LIMIT_BYTES),
        name="ffn",
    )(x, xs, norm3, w1, w3, w2)


def _normproj_body(x_ref, g_ref, w_ref, o_ref, xn_ref):
    @pl.when(pl.program_id(1) == 0)
    def _init():
        xn_ref[...] = (_rms(x_ref[...], EPS) * g_ref[...]).astype(BF16)

    o_ref[...] = _dot(xn_ref[...], w_ref[...].astype(BF16)).astype(o_ref.dtype)


def _normproj(x, norm3, norm_layer, w, layer, bm, bn):
    m, d = x.shape
    ncols = w.shape[-1]
    return pl.pallas_call(
        _normproj_body,
        grid=(m // bm, ncols // bn),
        in_specs=[
            pl.BlockSpec((bm, d), lambda i, j: (i, 0)),
            pl.BlockSpec((None, 1, d), lambda i, j: (norm_layer, 0, 0)),
            pl.BlockSpec((None, d, bn), lambda i, j: (layer, 0, j)),
        ],
        out_specs=pl.BlockSpec((bm, bn), lambda i, j: (i, j)),
        out_shape=jax.ShapeDtypeStruct((m, ncols), F32),
        scratch_shapes=[pltpu.VMEM((bm, d), BF16)],
        compiler_params=_cparams(2),
        name="normproj",
    )(x, norm3, w)


def _normproj_qkv_body(x_ref, g_ref, w_ref, qn_ref, kn_ref, c_ref, s_ref, q_ref, k_ref, v_ref, xn_ref, *, nj):
    j = pl.program_id(1)

    @pl.when(j == 0)
    def _init():
        xn_ref[...] = (_rms(x_ref[...], EPS) * g_ref[...]).astype(BF16)

    y = _dot(xn_ref[...], w_ref[...].astype(BF16))

    def norm_rope(hn_ref, o_ref):
        hn = hn_ref[...]
        c = c_ref[...]
        s = s_ref[...]
        for t in range(y.shape[-1] // DA_DH):
            sl = slice(t * DA_DH, (t + 1) * DA_DH)
            yn = _rms(y[:, sl], EPS) * hn
            o_ref[:, sl] = (yn * c + pltpu.roll(yn, DA_DH // 2, axis=1) * s).astype(o_ref.dtype)

    @pl.when(j < nj)
    def _q():
        norm_rope(qn_ref, q_ref)

    @pl.when((j >= nj) & (j < 2 * nj))
    def _k():
        norm_rope(kn_ref, k_ref)

    @pl.when(j >= 2 * nj)
    def _v():
        v_ref[...] = y


def _normproj_qkv(x, norm3, norm_layer, w, layer, bm, bn, q_dtype, qn, kn, c2, s2, table_blocks):
    m, d = x.shape
    nj = d // bn
    clip = lambda j, lo: jnp.clip(j - lo, 0, nj - 1)
    return pl.pallas_call(
        functools.partial(_normproj_qkv_body, nj=nj),
        grid=(m // bm, 3 * nj),
        in_specs=[
            pl.BlockSpec((bm, d), lambda i, j: (i, 0)),
            pl.BlockSpec((None, 1, d), lambda i, j: (norm_layer, 0, 0)),
            pl.BlockSpec((None, d, bn), lambda i, j: (layer, 0, j)),
            pl.BlockSpec((1, DA_DH), lambda i, j: (0, 0)),
            pl.BlockSpec((1, DA_DH), lambda i, j: (0, 0)),
            pl.BlockSpec((bm, DA_DH), lambda i, j: (i % table_blocks, 0)),
            pl.BlockSpec((bm, DA_DH), lambda i, j: (i % table_blocks, 0)),
        ],
        out_specs=[
            pl.BlockSpec((bm, bn), lambda i, j: (i, clip(j, 0))),
            pl.BlockSpec((bm, bn), lambda i, j: (i, clip(j, nj))),
            pl.BlockSpec((bm, bn), lambda i, j: (i, clip(j, 2 * nj))),
        ],
        out_shape=[
            jax.ShapeDtypeStruct((m, d), q_dtype),
            jax.ShapeDtypeStruct((m, d), F32),
            jax.ShapeDtypeStruct((m, d), F32),
        ],
        scratch_shapes=[pltpu.VMEM((bm, d), BF16)],
        compiler_params=_cparams(2),
        name="normproj_qkv",
    )(x, norm3, w, qn, kn, c2, s2)


def _matres_body(a_ref, w_ref, r_ref, o_ref):
    o_ref[...] = r_ref[...] + _dot(a_ref[...].astype(BF16), w_ref[...].astype(BF16))


def _matres(a, w, res, layer, bm, bn):
    m, k = a.shape
    n = w.shape[-1]
    return pl.pallas_call(
        _matres_body,
        grid=(m // bm, n // bn),
        in_specs=[
            pl.BlockSpec((bm, k), lambda i, j: (i, 0)),
            pl.BlockSpec((None, k, bn), lambda i, j: (layer, 0, j)),
            pl.BlockSpec((bm, bn), lambda i, j: (i, j)),
        ],
        out_specs=pl.BlockSpec((bm, bn), lambda i, j: (i, j)),
        out_shape=jax.ShapeDtypeStruct((m, n), F32),
        compiler_params=_cparams(2),
        name="matres",
    )(a, w, res)


def _retention_tables():
    expo = -5.0 - 7.0 * np.arange(R_HEADS, dtype=np.float64) / max(R_HEADS - 1, 1)
    log_g = np.log1p(-np.exp2(expo))
    idx = np.arange(R_CHUNK, dtype=np.float64)
    diff = idx[:, None] - idx[None, :]
    decay = np.where(diff >= 0, np.exp(log_g[:, None, None] * np.maximum(diff, 0.0)), 0.0)
    row_decay = np.exp(log_g[:, None] * (idx + 1.0))[:, :, None]
    k_decay = np.exp(log_g[:, None] * (R_CHUNK - 1.0 - idx))[:, :, None]
    state_decay = tuple(float(v) for v in np.exp(log_g * R_CHUNK))
    gamma = tuple(float(v) for v in np.exp(log_g))
    return (decay.astype(np.float32), row_decay.astype(np.float32), k_decay.astype(np.float32),
            state_decay, gamma)


def _rope_half(x, cos, sin):
    half = x.shape[-1] // 2
    x1 = x[:, :half]
    x2 = x[:, half:]
    return jnp.concatenate([x1 * cos - x2 * sin, x2 * cos + x1 * sin], axis=-1)


def _layer_norm(x, g, b):
    xc = x - jnp.mean(x, axis=-1, keepdims=True)
    return xc * lax.rsqrt(jnp.mean(xc * xc, axis=-1, keepdims=True) + LN_EPS) * g + b


def _even_prompt_body(p_ref, cos_ref, sin_ref, dec_ref, rd_ref, kd_ref, ws_ref, bs_ref, lng_ref, lnb_ref,
                      o_ref, st_ref, vn_ref, state, *, state_decay):
    c = pl.program_id(1)
    last = pl.num_programs(1) - 1

    @pl.when(c == 0)
    def _zero():
        state[...] = jnp.zeros_like(state)

    cos = cos_ref[...]
    sin = sin_ref[...]
    qk_w = R_HEADS * R_DK
    for h in range(R_HEADS):
        q = _rope_half(p_ref[:, h * R_DK:(h + 1) * R_DK], cos, sin)
        k = _rope_half(p_ref[:, qk_w + h * R_DK:qk_w + (h + 1) * R_DK], cos, sin) * (R_DK ** -0.5)
        v = p_ref[:, 2 * qk_w + h * R_DV:2 * qk_w + (h + 1) * R_DV].astype(BF16)
        g = p_ref[:, 3 * qk_w + h * R_DV:3 * qk_w + (h + 1) * R_DV]
        qb = q.astype(BF16)
        s = _dot_nt(qb, k.astype(BF16)) * dec_ref[h]
        st = state[h]
        o = _dot(s.astype(BF16), v) + _dot(qb, st.astype(BF16)) * rd_ref[h]
        kd_t = (k * kd_ref[h]).T.astype(BF16)
        state[h] = st * state_decay[h] + _dot(kd_t, v)
        o_ref[:, h * R_DV:(h + 1) * R_DV] = (_silu(g) * _rms(o, EPS)).astype(o_ref.dtype)

    u0 = 3 * qk_w + R_HEADS * R_DV
    cmw = CM_GROUPS * CM_GDIM
    vn = _layer_norm(_gelu_tanh(p_ref[:, u0 + cmw:u0 + 2 * cmw]), lng_ref[...], lnb_ref[...])
    row = lax.broadcasted_iota(jnp.int32, (R_CHUNK, R_CHUNK), 0)
    col = lax.broadcasted_iota(jnp.int32, (R_CHUNK, R_CHUNK), 1)
    for gi in range(CM_GROUPS):
        sl = slice(gi * CM_GDIM, (gi + 1) * CM_GDIM)
        w = jnp.where(row >= col, ws_ref[gi], 0.0).astype(BF16)
        mixed = _dot(w, vn[:, sl].astype(BF16)) + bs_ref[gi]
        u = p_ref[:, u0 + gi * CM_GDIM:u0 + (gi + 1) * CM_GDIM]
        o_ref[:, R_HEADS * R_DV + gi * CM_GDIM:R_HEADS * R_DV + (gi + 1) * CM_GDIM] = (
            _gelu_tanh(u) * mixed).astype(o_ref.dtype)

    @pl.when(c == last)
    def _emit():
        st_ref[...] = state[...]
        vn_ref[...] = vn


def _even_prompt(proj, cos, sin, tables, ws, bs4, lng3, lnb3, e, batch, seq):
    decay, row_decay, k_decay, state_decay, _ = tables
    nc = seq // R_CHUNK
    width = proj.shape[-1]
    cmw = CM_GROUPS * CM_GDIM
    out_w = R_HEADS * R_DV + cmw
    const3 = lambda b, c: (0, 0, 0)
    return pl.pallas_call(
        functools.partial(_even_prompt_body, state_decay=state_decay),
        grid=(batch, nc),
        in_specs=[
            pl.BlockSpec((R_CHUNK, width), lambda b, c: (b * nc + c, 0)),
            pl.BlockSpec((R_CHUNK, R_DK // 2), lambda b, c: (c, 0)),
            pl.BlockSpec((R_CHUNK, R_DK // 2), lambda b, c: (c, 0)),
            pl.BlockSpec(decay.shape, const3),
            pl.BlockSpec(row_decay.shape, const3),
            pl.BlockSpec(k_decay.shape, const3),
            pl.BlockSpec((None, CM_GROUPS, R_CHUNK, R_CHUNK), lambda b, c: (e, 0, 0, 0)),
            pl.BlockSpec((None, CM_GROUPS, R_CHUNK, 1), lambda b, c: (e, 0, 0, 0)),
            pl.BlockSpec((None, 1, cmw), lambda b, c: (e, 0, 0)),
            pl.BlockSpec((None, 1, cmw), lambda b, c: (e, 0, 0)),
        ],
        out_specs=[
            pl.BlockSpec((R_CHUNK, out_w), lambda b, c: (b * nc + c, 0)),
            pl.BlockSpec((None, R_HEADS, R_DK, R_DV), lambda b, c: (b, 0, 0, 0)),
            pl.BlockSpec((None, R_CHUNK, cmw), lambda b, c: (b, 0, 0)),
        ],
        out_shape=[
            jax.ShapeDtypeStruct((batch * seq, out_w), BF16),
            jax.ShapeDtypeStruct((batch, R_HEADS, R_DK, R_DV), F32),
            jax.ShapeDtypeStruct((batch, R_CHUNK, cmw), F32),
        ],
        scratch_shapes=[pltpu.VMEM((R_HEADS, R_DK, R_DV), F32)],
        compiler_params=_cparams(2),
        name="even_prompt",
    )(proj, cos, sin, jnp.asarray(decay), jnp.asarray(row_decay), jnp.asarray(k_decay), ws, bs4, lng3, lnb3)


def _even_sample_body(p_ref, cos_ref, sin_ref, st_in_ref, ws_ref, bs_ref, lng_ref, lnb_ref,
                      o_ref, st_ref, vn_ref, *, gamma):
    cos = cos_ref[...]
    sin = sin_ref[...]
    qk_w = R_HEADS * R_DK
    r0 = lax.broadcasted_iota(jnp.int32, (R_DK, R_DK), 0)
    r1 = lax.broadcasted_iota(jnp.int32, (R_DK, R_DK), 1)
    eye = jnp.where(r0 == r1, 1.0, 0.0).astype(BF16)
    for h in range(R_HEADS):
        q = _rope_half(p_ref[:, h * R_DK:(h + 1) * R_DK], cos, sin)
        k = _rope_half(p_ref[:, qk_w + h * R_DK:qk_w + (h + 1) * R_DK], cos, sin) * (R_DK ** -0.5)
        v = p_ref[:, 2 * qk_w + h * R_DV:2 * qk_w + (h + 1) * R_DV]
        g = p_ref[:, 3 * qk_w + h * R_DV:3 * qk_w + (h + 1) * R_DV]
        st = st_in_ref[h]
        s = jnp.sum(q * k, axis=-1, keepdims=True)
        qb = jnp.broadcast_to(q, (8, R_DK)).astype(BF16)
        cross = _dot(qb, st.astype(BF16))[0:1, :]
        o = s * v + cross * gamma[h]
        kb = jnp.broadcast_to(k, (128, R_DK)).astype(BF16)
        kcol = _dot_nt(eye, kb)
        kcol = jnp.concatenate([kcol] * (R_DV // 128), axis=-1)
        vb = v.astype(BF16).astype(F32)
        st_ref[h] = st * gamma[h] + kcol * vb
        o_ref[:, h * R_DV:(h + 1) * R_DV] = (_silu(g) * _rms(o, EPS)).astype(o_ref.dtype)

    u0 = 3 * qk_w + R_HEADS * R_DV
    cmw = CM_GROUPS * CM_GDIM
    vn = _layer_norm(_gelu_tanh(p_ref[:, u0 + cmw:u0 + 2 * cmw]), lng_ref[...], lnb_ref[...])
    vn_ref[...] = vn
    for gi in range(CM_GROUPS):
        sl = slice(gi * CM_GDIM, (gi + 1) * CM_GDIM)
        mixed = ws_ref[gi, 0:1, 0:1] * vn[:, sl] + bs_ref[gi, 0:1, :]
        u = p_ref[:, u0 + gi * CM_GDIM:u0 + (gi + 1) * CM_GDIM]
        o_ref[:, R_HEADS * R_DV + gi * CM_GDIM:R_HEADS * R_DV + (gi + 1) * CM_GDIM] = (
            _gelu_tanh(u) * mixed).astype(o_ref.dtype)


def _even_sample(proj3, cos, sin, tables, st_in, ws, bs4, lng3, lnb3, e):
    gamma = tables[4]
    nb = proj3.shape[0]
    width = proj3.shape[-1]
    cmw = CM_GROUPS * CM_GDIM
    out_w = R_HEADS * R_DV + cmw
    return pl.pallas_call(
        functools.partial(_even_sample_body, gamma=gamma),
        grid=(nb,),
        in_specs=[
            pl.BlockSpec((None, 1, width), lambda b: (b, 0, 0)),
            pl.BlockSpec((1, R_DK // 2), lambda b: (0, 0)),
            pl.BlockSpec((1, R_DK // 2), lambda b: (0, 0)),
            pl.BlockSpec((None, None, R_HEADS, R_DK, R_DV), lambda b: (e, b, 0, 0, 0)),
            pl.BlockSpec((None, CM_GROUPS, R_CHUNK, R_CHUNK), lambda b: (e, 0, 0, 0)),
            pl.BlockSpec((None, CM_GROUPS, R_CHUNK, 1), lambda b: (e, 0, 0, 0)),
            pl.BlockSpec((None, 1, cmw), lambda b: (e, 0, 0)),
            pl.BlockSpec((None, 1, cmw), lambda b: (e, 0, 0)),
        ],
        out_specs=[
            pl.BlockSpec((None, 1, out_w), lambda b: (b, 0, 0)),
            pl.BlockSpec((None, R_HEADS, R_DK, R_DV), lambda b: (b, 0, 0, 0)),
            pl.BlockSpec((None, 1, cmw), lambda b: (b, 0, 0)),
        ],
        out_shape=[
            jax.ShapeDtypeStruct((nb, 1, out_w), F32),
            jax.ShapeDtypeStruct((nb, R_HEADS, R_DK, R_DV), F32),
            jax.ShapeDtypeStruct((nb, 1, cmw), F32),
        ],
        compiler_params=_cparams(1),
        name="even_sample",
    )(proj3, cos, sin, st_in, ws, bs4, lng3, lnb3)


def _diff_lambda(lq1_ref, lk1_ref, lq2_ref, lk2_ref, lam_init):
    a = jnp.sum(lq1_ref[...] * lk1_ref[...], axis=-1, keepdims=True)
    b = jnp.sum(lq2_ref[...] * lk2_ref[...], axis=-1, keepdims=True)
    return jnp.exp(a) - jnp.exp(b) + lam_init


def _flash_body(qi_tab, ki_tab, qt_ref, k_ref, v_ref, lq1_ref, lk1_ref, lq2_ref, lk2_ref, sub_ref,
                o_ref, m_ref, acc_ref, *, lam_init):
    t = pl.program_id(2)
    qi = qi_tab[t]
    ki = ki_tab[t]
    tq = qt_ref.shape[1]
    tk = k_ref.shape[0]
    r = tq // tk

    @pl.when(ki == 0)
    def _init():
        m_ref[...] = jnp.full_like(m_ref, -jnp.inf)
        acc_ref[...] = jnp.zeros_like(acc_ref)

    def update(first, on_diag):
        cols = slice(first * tk, tq)
        w = tq - first * tk
        vt = jnp.concatenate([v_ref[...].T, jnp.ones((BF16_SUBLANES, tk), F32)], axis=0).astype(BF16)
        if on_diag:
            key = lax.broadcasted_iota(jnp.int32, (tk, w), 0)
            qry = lax.broadcasted_iota(jnp.int32, (tk, w), 1)
            keep = key <= qry
        sts = []
        for half in range(2):
            sl = slice(half * DA_DH, (half + 1) * DA_DH)
            st = _dot(k_ref[:, sl].astype(BF16), qt_ref[sl, cols])
            sts.append(jnp.where(keep, st, -jnp.inf) if on_diag else st)
        st = jnp.concatenate(sts, axis=1)
        m_old = jnp.concatenate([m_ref[0, :, cols], m_ref[1, :, cols]], axis=1)
        m_new = jnp.maximum(m_old, jnp.max(st, axis=0, keepdims=True))
        alpha = jnp.exp2(m_old - m_new)
        pt = jnp.exp2((st - m_new).astype(BF16))
        acc = alpha * jnp.concatenate([acc_ref[0, :, cols], acc_ref[1, :, cols]], axis=1) + _dot(vt, pt)
        for half in range(2):
            acc_ref[half, :, cols] = acc[:, half * w:(half + 1) * w]
            m_ref[half, :, cols] = m_new[:, half * w:(half + 1) * w]

    @pl.when(ki < qi * r)
    def _below_diag():
        update(0, False)

    for kd in range(r):
        @pl.when(ki == qi * r + kd)
        def _on_diag(kd=kd):
            update(kd, True)

    @pl.when(ki == qi * r + (r - 1))
    def _finish():
        lam = _diff_lambda(lq1_ref, lk1_ref, lq2_ref, lk2_ref, lam_init)
        a0 = acc_ref[0]
        a1 = acc_ref[1]
        ot = (a0[0:DA_VDIM] * (1.0 / a0[DA_VDIM:DA_VDIM + 1])
              - lam * (a1[0:DA_VDIM] * (1.0 / a1[DA_VDIM:DA_VDIM + 1])))
        ot = ot * lax.rsqrt(jnp.mean(ot * ot, axis=0, keepdims=True) + SUBLN_EPS)
        o_ref[...] = (ot.T * sub_ref[...] * (1.0 - lam_init)).astype(o_ref.dtype)


def _flash_prompt(q_t, k, v, lam_vecs, subln, batch, seq, lam_init, tq, tk):
    nq = seq // tq
    r = tq // tk
    qi_tab = np.concatenate([np.full((i + 1) * r, i, np.int32) for i in range(nq)])
    ki_tab = np.concatenate([np.arange((i + 1) * r, dtype=np.int32) for i in range(nq)])
    n_tri = int(qi_tab.shape[0])
    vec_spec = pl.BlockSpec((1, DA_DH), lambda b, h, t, qt, kt: (0, 0))
    grid_spec = pltpu.PrefetchScalarGridSpec(
        num_scalar_prefetch=2,
        grid=(batch, DA_HEADS, n_tri),
        in_specs=[
            pl.BlockSpec((DA_VDIM, tq), lambda b, h, t, qt, kt: (h, b * nq + qt[t])),
            pl.BlockSpec((tk, DA_VDIM), lambda b, h, t, qt, kt: (b * nq * r + kt[t], h)),
            pl.BlockSpec((tk, DA_VDIM), lambda b, h, t, qt, kt: (b * nq * r + kt[t], h)),
            vec_spec, vec_spec, vec_spec, vec_spec,
            pl.BlockSpec((1, DA_VDIM), lambda b, h, t, qt, kt: (0, 0)),
        ],
        out_specs=pl.BlockSpec((tq, DA_VDIM), lambda b, h, t, qt, kt: (b * nq + qt[t], h)),
        scratch_shapes=[
            pltpu.VMEM((2, 1, tq), F32),
            pltpu.VMEM((2, DA_VDIM + BF16_SUBLANES, tq), F32),
        ],
    )
    return pl.pallas_call(
        functools.partial(_flash_body, lam_init=lam_init),
        grid_spec=grid_spec,
        out_shape=jax.ShapeDtypeStruct((batch * seq, DA_HEADS * DA_VDIM), BF16),
        compiler_params=_cparams(3),
        name="diff_attn_prompt",
    )(jnp.asarray(qi_tab), jnp.asarray(ki_tab), q_t, k, v, *lam_vecs, subln)


def _decode_body(pt_ref, q_ref, kn_ref, vn_ref, *rest, scale, lam_init, pages_per_step):
    k_refs = rest[:pages_per_step]
    v_refs = rest[pages_per_step:2 * pages_per_step]
    (own_ref, lq1_ref, lk1_ref, lq2_ref, lk2_ref, sub_ref,
     o_ref, m_ref, l_ref, acc_ref) = rest[2 * pages_per_step:]
    step = pl.program_id(1)
    n_hh = 2 * DA_HEADS
    rows_v = v_refs[0].shape[0]

    @pl.when(step == 0)
    def _init():
        m_ref[...] = jnp.full_like(m_ref, -jnp.inf)
        l_ref[...] = jnp.zeros_like(l_ref)
        acc_ref[...] = jnp.zeros_like(acc_ref)

    q = q_ref[...]
    parity = lax.broadcasted_iota(jnp.int32, (n_hh, DA_DH), 0) & 1
    q2 = jnp.concatenate([jnp.where(parity == 0, q, 0.0), jnp.where(parity == 1, q, 0.0)], axis=1).astype(BF16)
    own = own_ref[...] > 0.0
    s_parts = []
    for kr in k_refs:
        k2 = jnp.concatenate([kr[pl.ds(0, rows_v, stride=2), :], kr[pl.ds(1, rows_v, stride=2), :]],
                             axis=1).astype(BF16)
        s_parts.append(jnp.where(own, _dot_nt(q2, k2), -jnp.inf))
    s = jnp.concatenate(s_parts, axis=1) * scale
    m_old = m_ref[...]
    m_new = jnp.maximum(m_old, jnp.max(s, axis=-1, keepdims=True))
    alpha = jnp.exp(m_old - m_new)
    p = jnp.exp(s - m_new)
    l_ref[...] = alpha * l_ref[...] + jnp.sum(p, axis=-1, keepdims=True)
    pb = p.astype(BF16)
    pv = jnp.zeros((n_hh, DA_VDIM), F32)
    for r, vr in enumerate(v_refs):
        pv += _dot(pb[:, r * rows_v:(r + 1) * rows_v], vr[...].astype(BF16))
    acc_ref[...] = alpha * acc_ref[...] + pv
    m_ref[...] = m_new

    @pl.when(step == pl.num_programs(1) - 1)
    def _finish():
        s_new = jnp.sum(q_ref[...] * kn_ref[...], axis=-1, keepdims=True) * scale
        m_old2 = m_ref[...]
        m_fin = jnp.maximum(m_old2, s_new)
        alpha2 = jnp.exp(m_old2 - m_fin)
        p_new = jnp.exp(s_new - m_fin)
        l_fin = alpha2 * l_ref[...] + p_new
        a = (alpha2 * acc_ref[...] + p_new * vn_ref[...]) / l_fin
        lam = _diff_lambda(lq1_ref, lk1_ref, lq2_ref, lk2_ref, lam_init)
        for h in range(DA_HEADS):
            o = a[2 * h:2 * h + 1, :] - lam * a[2 * h + 1:2 * h + 2, :]
            o_ref[:, h * DA_VDIM:(h + 1) * DA_VDIM] = (
                _rms(o, SUBLN_EPS) * sub_ref[...] * (1.0 - lam_init)).astype(o_ref.dtype)


def _decode_attn(q3, kn3, vn3, cache_k, cache_v, page_table, lam_vecs, subln, layer, lam_init, pages_per_step):
    nb = q3.shape[0]
    n_hh = 2 * DA_HEADS
    n_layers, n_pool, page = cache_k.shape[:3]
    n_pages = page_table.shape[1]
    n_steps = n_pages // pages_per_step
    cache_k = cache_k.reshape(n_layers, n_pool, page * n_hh, DA_DH)
    cache_v = cache_v.reshape(n_layers, n_pool, page * DA_HEADS, DA_VDIM)

    def page_spec(r, heads, dim):
        return pl.BlockSpec((None, None, page * heads, dim),
                            lambda b, s, pt: (layer, pt[b, s * pages_per_step + r], 0, 0))

    rows_v = np.arange(page * DA_HEADS)
    own = (rows_v[None, :] % DA_HEADS == np.arange(n_hh)[:, None] // 2).astype(np.float32)

    vec_spec = pl.BlockSpec((1, DA_DH), lambda b, s, pt: (0, 0))
    grid_spec = pltpu.PrefetchScalarGridSpec(
        num_scalar_prefetch=1,
        grid=(nb, n_steps),
        in_specs=[pl.BlockSpec((None, n_hh, DA_DH), lambda b, s, pt: (b, 0, 0)),
                  pl.BlockSpec((None, n_hh, DA_DH), lambda b, s, pt: (b, 0, 0)),
                  pl.BlockSpec((None, n_hh, DA_VDIM), lambda b, s, pt: (b, 0, 0))]
        + [page_spec(r, n_hh, DA_DH) for r in range(pages_per_step)]
        + [page_spec(r, DA_HEADS, DA_VDIM) for r in range(pages_per_step)]
        + [pl.BlockSpec(own.shape, lambda b, s, pt: (0, 0))]
        + [vec_spec, vec_spec, vec_spec, vec_spec,
           pl.BlockSpec((1, DA_VDIM), lambda b, s, pt: (0, 0))],
        out_specs=pl.BlockSpec((None, 1, DA_HEADS * DA_VDIM), lambda b, s, pt: (b, 0, 0)),
        scratch_shapes=[
            pltpu.VMEM((n_hh, 1), F32),
            pltpu.VMEM((n_hh, 1), F32),
            pltpu.VMEM((n_hh, DA_VDIM), F32),
        ],
    )
    return pl.pallas_call(
        functools.partial(_decode_body, scale=DA_DH ** -0.5, lam_init=lam_init, pages_per_step=pages_per_step),
        grid_spec=grid_spec,
        out_shape=jax.ShapeDtypeStruct((nb, 1, DA_HEADS * DA_VDIM), F32),
        compiler_params=_cparams(2),
        name="diff_attn_decode",
    )(page_table, q3, kn3, vn3, *([cache_k] * pages_per_step), *([cache_v] * pages_per_step),
      jnp.asarray(own), *lam_vecs, subln)


def _rope_tables(pos, d):
    inv = ROPE_THETA ** (-jnp.arange(0, d, 2, dtype=F32) / d)
    ang = pos.astype(F32)[:, None] * inv[None, :]
    return jnp.cos(ang), jnp.sin(ang)


def kernel(x_prompt, x_sample, state_ret, cache_k, cache_v, page_table, ffn_a_norm, ffn_a_w1, ffn_a_w3, ffn_a_w2, mix_norm, ffn_b_norm, ffn_b_w1, ffn_b_w3, ffn_b_w2, ev_w_in, ev_w_out, cm_ws, cm_bs, cm_ln_g, cm_ln_b, od_w_in, od_w_out, da_q_norm, da_k_norm, da_lam_q1, da_lam_k1, da_lam_q2, da_lam_k2, da_subln):
    batch, seq, d = x_prompt.shape
    nb, t_new, _ = x_sample.shape
    depth = ffn_a_norm.shape[0]
    assert t_new == 1 and seq % 1024 == 0 and nb == 8

    bm_p = 1024
    pos_p = jnp.arange(seq, dtype=jnp.int32)
    pos_s = PAST_LEN + jnp.arange(t_new, dtype=jnp.int32)
    cos_e, sin_e = _rope_tables(pos_p, R_DK)
    cos_es, sin_es = _rope_tables(pos_s, R_DK)
    cos_o, sin_o = _rope_tables(pos_p, DA_DH)
    cos_os, sin_os = _rope_tables(pos_s, DA_DH)
    c2_p = jnp.concatenate([cos_o, cos_o], axis=-1)
    s2_p = jnp.concatenate([-sin_o, sin_o], axis=-1)
    c2_s = jnp.broadcast_to(jnp.concatenate([cos_os, cos_os], axis=-1), (nb, DA_DH))
    s2_s = jnp.broadcast_to(jnp.concatenate([-sin_os, sin_os], axis=-1), (nb, DA_DH))
    tables = _retention_tables()

    as3 = lambda a: a.reshape(a.shape[0], 1, a.shape[1])
    ffn_a_norm3, ffn_b_norm3, mix_norm3 = as3(ffn_a_norm), as3(ffn_b_norm), as3(mix_norm)
    lng3, lnb3 = as3(cm_ln_g), as3(cm_ln_b)
    bs4 = cm_bs.reshape(cm_bs.shape + (1,))

    xp = x_prompt.reshape(batch * seq, d)
    xs = x_sample.reshape(nb * t_new, d)
    ret_p, ret_s, cmv_p, cmv_s = [], [], [], []
    kp_l, vp_l, ks_l, vs_l = [], [], [], []

    def ffn_pair(xp, xs, norm3, w1, w3, w2, l):
        return _ffn(xp, xs, norm3, w1, w3, w2, l, bm=bm_p, bf=512)

    for l in range(depth):
        xp, xs = ffn_pair(xp, xs, ffn_a_norm3, ffn_a_w1, ffn_a_w3, ffn_a_w2, l)
        if l % 2 == 0:
            e = l // 2
            even_in = ev_w_in.shape[-1]
            proj_p = _normproj(xp, mix_norm3, l, ev_w_in, e, bm_p, 1024)
            mix_p, st_p, vn_p = _even_prompt(proj_p, cos_e, sin_e, tables, cm_ws, bs4, lng3, lnb3, e, batch, seq)
            proj_s = _normproj(xs, mix_norm3, l, ev_w_in, e, nb, 1024)
            mix_s, st_s, vn_s = _even_sample(proj_s.reshape(nb, 1, even_in), cos_es, sin_es, tables,
                                             state_ret, cm_ws, bs4, lng3, lnb3, e)
            xp = _matres(mix_p, ev_w_out, xp, e, bm_p, 1024)
            xs = _matres(mix_s.reshape(nb, -1), ev_w_out, xs, e, nb, 2048)
            ret_p.append(st_p)
            ret_s.append(st_s)
            cmv_p.append(vn_p)
            cmv_s.append(vn_s)
        else:
            o = l // 2
            lam_init = 0.8 - 0.6 * math.exp(-0.3 * l)
            qn = da_q_norm[o].reshape(1, DA_DH)
            kn = da_k_norm[o].reshape(1, DA_DH)
            lam_vecs = [a[o].reshape(1, DA_DH) for a in (da_lam_q1, da_lam_k1, da_lam_q2, da_lam_k2)]
            subln = da_subln[o].reshape(1, DA_VDIM)
            tb = seq // bm_p
            q_s, k_s, v_s = _normproj_qkv(xs, mix_norm3, l, od_w_in, o, nb, 1024, F32, qn, kn, c2_s, s2_s, 1)
            v_rep = jnp.repeat(v_s.reshape(nb, DA_HEADS, DA_VDIM), 2, axis=1)
            att_s = _decode_attn(q_s.reshape(nb, 2 * DA_HEADS, DA_DH), k_s.reshape(nb, 2 * DA_HEADS, DA_DH), v_rep,
                                 cache_k, cache_v, page_table, lam_vecs, subln, o, lam_init, pages_per_step=8)
            xs = _matres(att_s.reshape(nb, d), od_w_out, xs, o, nb, 2048)
            qn_scaled = qn * (DA_DH ** -0.5 * math.log2(math.e))
            q_p, k_p, v_p = _normproj_qkv(xp, mix_norm3, l, od_w_in, o, bm_p, 512, BF16, qn_scaled, kn, c2_p, s2_p, tb)
            att_p = _flash_prompt(q_p.T, k_p, v_p, lam_vecs, subln, batch, seq, lam_init, tq=1024, tk=512)
            xp = _matres(att_p, od_w_out, xp, o, bm_p, 1024)
            kp_l.append(k_p.reshape(batch, seq, 2 * DA_HEADS, DA_DH))
            vp_l.append(v_p.reshape(batch, seq, DA_HEADS, DA_VDIM))
            ks_l.append(k_s.reshape(nb, t_new, 2 * DA_HEADS, DA_DH))
            vs_l.append(v_s.reshape(nb, t_new, DA_HEADS, DA_VDIM))
        xp, xs = ffn_pair(xp, xs, ffn_b_norm3, ffn_b_w1, ffn_b_w3, ffn_b_w2, l)

    return (xp.reshape(batch, seq, d), xs.reshape(nb, t_new, d),
            jnp.stack(ret_p), jnp.stack(ret_s), jnp.stack(cmv_p), jnp.stack(cmv_s),
            jnp.stack(kp_l), jnp.stack(vp_l), jnp.stack(ks_l), jnp.stack(vs_l))
```

```python
import functools
import math

import numpy as np
import jax
import jax.numpy as jnp
from jax import lax
from jax.experimental import pallas as pl
from jax.experimental.pallas import tpu as pltpu

F32 = jnp.float32
BF16 = jnp.bfloat16

PAST_LEN = 16384
R_HEADS = 4
R_DK = 256
R_DV = 256
R_CHUNK = 128
CM_GROUPS = 4
CM_GDIM = 256
DA_HEADS = 8
DA_DH = 128
DA_VDIM = 256
LOG2_DA_DH = 7
LOG2_DA_VDIM = 8
ROPE_THETA = 10000.0
EPS = 1e-6
SUBLN_EPS = 1e-5
LN_EPS = 1e-5

BF16_SUBLANES = 16
VMEM_LIMIT_BYTES = 56 * 1024 * 1024
FFN_VMEM_LIMIT_BYTES = 60000 * 1024


def _cparams(n_axes, vmem_limit_bytes=VMEM_LIMIT_BYTES):
    return pltpu.CompilerParams(
        dimension_semantics=("arbitrary",) * n_axes,
        vmem_limit_bytes=vmem_limit_bytes,
    )


def _silu(a):
    return a / (1.0 + jnp.exp(-a))


def _gelu_tanh(x):
    c = math.sqrt(2.0 / math.pi)
    return x * (0.5 * (1.0 + jnp.tanh(c * (x + 0.044715 * (x * x * x)))))


def _rms(x, eps):
    return x * lax.rsqrt(jnp.mean(x * x, axis=-1, keepdims=True) + eps)


def _dot(a, b):
    return jnp.dot(a, b, preferred_element_type=F32)


def _dot_nt(a, b):
    return lax.dot_general(a, b, (((1,), (1,)), ((), ())), preferred_element_type=F32)


def _ffn_body(x_ref, xs_ref, g_ref, w1_ref, w3_ref, w2_ref, o_ref, os_ref, xn_ref, *, n_split):
    i = pl.program_id(0)
    j = pl.program_id(1)
    bm = x_ref.shape[0]
    ms = xs_ref.shape[0]
    tail = xn_ref.shape[0] - bm

    @pl.when(j == 0)
    def _init():
        x = x_ref[...]
        xn_ref[0:bm, :] = (_rms(x, EPS) * g_ref[...]).astype(BF16)
        o_ref[...] = x

    @pl.when((i == 0) & (j == 0))
    def _init_sample():
        xs = xs_ref[...]
        xsn = jnp.concatenate([_rms(xs, EPS) * g_ref[...], jnp.zeros((tail - ms, xs.shape[1]), F32)], axis=0)
        xn_ref[bm:bm + tail, :] = xsn.astype(BF16)
        os_ref[...] = xs

    xn = xn_ref[...]
    a = _dot(xn, w1_ref[...].astype(BF16))
    b = _dot(xn, w3_ref[...].astype(BF16))
    h = ((_silu(a) * b) * 0.5).astype(BF16)
    d = o_ref.shape[-1]
    w = d // n_split
    for n in range(n_split):
        sl = slice(n * w, (n + 1) * w)
        y = _dot(h, w2_ref[:, sl].astype(BF16))
        o_ref[:, sl] += y[0:bm]
        os_ref[:, sl] += jnp.where(i == 0, y[bm:bm + ms], 0.0)


def _ffn(x, xs, norm3, w1, w3, w2, layer, bm, bf):
    m, d = x.shape
    ms = xs.shape[0]
    ff = w1.shape[-1]
    return pl.pallas_call(
        functools.partial(_ffn_body, n_split=4),
        grid=(m // bm, ff // bf),
        in_specs=[
            pl.BlockSpec((bm, d), lambda i, j: (i, 0), pipeline_mode=pl.Buffered(1)),
            pl.BlockSpec((ms, d), lambda i, j: (0, 0)),
            pl.BlockSpec((None, 1, d), lambda i, j: (layer, 0, 0)),
            pl.BlockSpec((None, d, bf), lambda i, j: (layer, 0, j)),
            pl.BlockSpec((None, d, bf), lambda i, j: (layer, 0, j)),
            pl.BlockSpec((None, bf, d), lambda i, j: (layer, j, 0)),
        ],
        out_specs=[
            pl.BlockSpec((bm, d), lambda i, j: (i, 0)),
            pl.BlockSpec((ms, d), lambda i, j: (0, 0)),
        ],
        out_shape=[jax.ShapeDtypeStruct((m, d), F32), jax.ShapeDtypeStruct((ms, d), F32)],
        scratch_shapes=[pltpu.VMEM((bm + BF16_SUBLANES, d), BF16)],
        compiler_params=_cparams(2, FFN_VMEM_LIMIT_BYTES),
        name="ffn",
    )(x, xs, norm3, w1, w3, w2)


def _normproj_body(x_ref, g_ref, w_ref, o_ref, xn_ref):
    @pl.when(pl.program_id(1) == 0)
    def _init():
        xn_ref[...] = (_rms(x_ref[...], EPS) * g_ref[...]).astype(BF16)

    o_ref[...] = _dot(xn_ref[...], w_ref[...].astype(BF16)).astype(o_ref.dtype)


def _normproj(x, norm3, norm_layer, w, layer, bm, bn):
    m, d = x.shape
    ncols = w.shape[-1]
    return pl.pallas_call(
        _normproj_body,
        grid=(m // bm, ncols // bn),
        in_specs=[
            pl.BlockSpec((bm, d), lambda i, j: (i, 0)),
            pl.BlockSpec((None, 1, d), lambda i, j: (norm_layer, 0, 0)),
            pl.BlockSpec((None, d, bn), lambda i, j: (layer, 0, j)),
        ],
        out_specs=pl.BlockSpec((bm, bn), lambda i, j: (i, j)),
        out_shape=jax.ShapeDtypeStruct((m, ncols), F32),
        scratch_shapes=[pltpu.VMEM((bm, d), BF16)],
        compiler_params=_cparams(2),
        name="normproj",
    )(x, norm3, w)


def _normproj_qkv_body(x_ref, g_ref, w_ref, qn_ref, kn_ref, c_ref, s_ref, q_ref, k_ref, v_ref, xn_ref, *, nj):
    j = pl.program_id(1)

    @pl.when(j == 0)
    def _init():
        xn_ref[...] = (_rms(x_ref[...], EPS) * g_ref[...]).astype(BF16)

    y = _dot(xn_ref[...], w_ref[...].astype(BF16))

    def norm_rope(hn_ref, o_ref):
        hn = hn_ref[...]
        c = c_ref[...]
        s = s_ref[...]
        for t in range(y.shape[-1] // DA_DH):
            sl = slice(t * DA_DH, (t + 1) * DA_DH)
            yn = _rms(y[:, sl], EPS) * hn
            o_ref[:, sl] = (yn * c + pltpu.roll(yn, DA_DH // 2, axis=1) * s).astype(o_ref.dtype)

    @pl.when(j < nj)
    def _q():
        norm_rope(qn_ref, q_ref)

    @pl.when((j >= nj) & (j < 2 * nj))
    def _k():
        norm_rope(kn_ref, k_ref)

    @pl.when(j >= 2 * nj)
    def _v():
        v_ref[...] = y


def _normproj_qkv(x, norm3, norm_layer, w, layer, bm, bn, q_dtype, qn, kn, c2, s2, table_blocks):
    m, d = x.shape
    nj = d // bn
    clip = lambda j, lo: jnp.clip(j - lo, 0, nj - 1)
    return pl.pallas_call(
        functools.partial(_normproj_qkv_body, nj=nj),
        grid=(m // bm, 3 * nj),
        in_specs=[
            pl.BlockSpec((bm, d), lambda i, j: (i, 0)),
            pl.BlockSpec((None, 1, d), lambda i, j: (norm_layer, 0, 0)),
            pl.BlockSpec((None, d, bn), lambda i, j: (layer, 0, j)),
            pl.BlockSpec((1, DA_DH), lambda i, j: (0, 0)),
            pl.BlockSpec((1, DA_DH), lambda i, j: (0, 0)),
            pl.BlockSpec((bm, DA_DH), lambda i, j: (i % table_blocks, 0)),
            pl.BlockSpec((bm, DA_DH), lambda i, j: (i % table_blocks, 0)),
        ],
        out_specs=[
            pl.BlockSpec((bm, bn), lambda i, j: (i, clip(j, 0))),
            pl.BlockSpec((bm, bn), lambda i, j: (i, clip(j, nj))),
            pl.BlockSpec((bm, bn), lambda i, j: (i, clip(j, 2 * nj))),
        ],
        out_shape=[
            jax.ShapeDtypeStruct((m, d), q_dtype),
            jax.ShapeDtypeStruct((m, d), F32),
            jax.ShapeDtypeStruct((m, d), F32),
        ],
        scratch_shapes=[pltpu.VMEM((bm, d), BF16)],
        compiler_params=_cparams(2),
        name="normproj_qkv",
    )(x, norm3, w, qn, kn, c2, s2)


def _matres_body(a_ref, w_ref, r_ref, o_ref):
    o_ref[...] = r_ref[...] + _dot(a_ref[...].astype(BF16), w_ref[...].astype(BF16))


def _matres(a, w, res, layer, bm, bn):
    m, k = a.shape
    n = w.shape[-1]
    return pl.pallas_call(
        _matres_body,
        grid=(m // bm, n // bn),
        in_specs=[
            pl.BlockSpec((bm, k), lambda i, j: (i, 0)),
            pl.BlockSpec((None, k, bn), lambda i, j: (layer, 0, j)),
            pl.BlockSpec((bm, bn), lambda i, j: (i, j)),
        ],
        out_specs=pl.BlockSpec((bm, bn), lambda i, j: (i, j)),
        out_shape=jax.ShapeDtypeStruct((m, n), F32),
        compiler_params=_cparams(2),
        name="matres",
    )(a, w, res)


def _retention_tables():
    expo = -5.0 - 7.0 * np.arange(R_HEADS, dtype=np.float64) / max(R_HEADS - 1, 1)
    log_g = np.log1p(-np.exp2(expo))
    idx = np.arange(R_CHUNK, dtype=np.float64)
    diff = idx[:, None] - idx[None, :]
    decay = np.where(diff >= 0, np.exp(log_g[:, None, None] * np.maximum(diff, 0.0)), 0.0)
    row_decay = np.exp(log_g[:, None] * (idx + 1.0))[:, :, None]
    k_decay = np.exp(log_g[:, None] * (R_CHUNK - 1.0 - idx))[:, :, None]
    state_decay = tuple(float(v) for v in np.exp(log_g * R_CHUNK))
    gamma = tuple(float(v) for v in np.exp(log_g))
    return (decay.astype(np.float32), row_decay.astype(np.float32), k_decay.astype(np.float32),
            state_decay, gamma)


def _rope_half(x, cos, sin):
    half = x.shape[-1] // 2
    x1 = x[:, :half]
    x2 = x[:, half:]
    return jnp.concatenate([x1 * cos - x2 * sin, x2 * cos + x1 * sin], axis=-1)


def _layer_norm(x, g, b):
    xc = x - jnp.mean(x, axis=-1, keepdims=True)
    return xc * lax.rsqrt(jnp.mean(xc * xc, axis=-1, keepdims=True) + LN_EPS) * g + b


def _even_prompt_body(p_ref, cos_ref, sin_ref, dec_ref, rd_ref, kd_ref, ws_ref, bs_ref, lng_ref, lnb_ref,
                      o_ref, st_ref, vn_ref, state, *, state_decay):
    c = pl.program_id(1)
    last = pl.num_programs(1) - 1

    @pl.when(c == 0)
    def _zero():
        state[...] = jnp.zeros_like(state)

    cos = cos_ref[...]
    sin = sin_ref[...]
    qk_w = R_HEADS * R_DK
    for h in range(R_HEADS):
        q = _rope_half(p_ref[:, h * R_DK:(h + 1) * R_DK], cos, sin)
        k = _rope_half(p_ref[:, qk_w + h * R_DK:qk_w + (h + 1) * R_DK], cos, sin) * (R_DK ** -0.5)
        v = p_ref[:, 2 * qk_w + h * R_DV:2 * qk_w + (h + 1) * R_DV].astype(BF16)
        g = p_ref[:, 3 * qk_w + h * R_DV:3 * qk_w + (h + 1) * R_DV]
        qb = q.astype(BF16)
        s = _dot_nt(qb, k.astype(BF16)) * dec_ref[h]
        st = state[h]
        o = _dot(s.astype(BF16), v) + _dot(qb, st.astype(BF16)) * rd_ref[h]
        kd_t = (k * kd_ref[h]).T.astype(BF16)
        state[h] = st * state_decay[h] + _dot(kd_t, v)
        o_ref[:, h * R_DV:(h + 1) * R_DV] = (_silu(g) * _rms(o, EPS)).astype(o_ref.dtype)

    u0 = 3 * qk_w + R_HEADS * R_DV
    cmw = CM_GROUPS * CM_GDIM
    vn = _layer_norm(_gelu_tanh(p_ref[:, u0 + cmw:u0 + 2 * cmw]), lng_ref[...], lnb_ref[...])
    row = lax.broadcasted_iota(jnp.int32, (R_CHUNK, R_CHUNK), 0)
    col = lax.broadcasted_iota(jnp.int32, (R_CHUNK, R_CHUNK), 1)
    for gi in range(CM_GROUPS):
        sl = slice(gi * CM_GDIM, (gi + 1) * CM_GDIM)
        w = jnp.where(row >= col, ws_ref[gi], 0.0).astype(BF16)
        mixed = _dot(w, vn[:, sl].astype(BF16)) + bs_ref[gi]
        u = p_ref[:, u0 + gi * CM_GDIM:u0 + (gi + 1) * CM_GDIM]
        o_ref[:, R_HEADS * R_DV + gi * CM_GDIM:R_HEADS * R_DV + (gi + 1) * CM_GDIM] = (
            _gelu_tanh(u) * mixed).astype(o_ref.dtype)

    @pl.when(c == last)
    def _emit():
        st_ref[...] = state[...]
        vn_ref[...] = vn


def _even_prompt(proj, cos, sin, tables, ws, bs4, lng3, lnb3, e, batch, seq):
    decay, row_decay, k_decay, state_decay, _ = tables
    nc = seq // R_CHUNK
    width = proj.shape[-1]
    cmw = CM_GROUPS * CM_GDIM
    out_w = R_HEADS * R_DV + cmw
    const3 = lambda b, c: (0, 0, 0)
    return pl.pallas_call(
        functools.partial(_even_prompt_body, state_decay=state_decay),
        grid=(batch, nc),
        in_specs=[
            pl.BlockSpec((R_CHUNK, width), lambda b, c: (b * nc + c, 0)),
            pl.BlockSpec((R_CHUNK, R_DK // 2), lambda b, c: (c, 0)),
            pl.BlockSpec((R_CHUNK, R_DK // 2), lambda b, c: (c, 0)),
            pl.BlockSpec(decay.shape, const3),
            pl.BlockSpec(row_decay.shape, const3),
            pl.BlockSpec(k_decay.shape, const3),
            pl.BlockSpec((None, CM_GROUPS, R_CHUNK, R_CHUNK), lambda b, c: (e, 0, 0, 0)),
            pl.BlockSpec((None, CM_GROUPS, R_CHUNK, 1), lambda b, c: (e, 0, 0, 0)),
            pl.BlockSpec((None, 1, cmw), lambda b, c: (e, 0, 0)),
            pl.BlockSpec((None, 1, cmw), lambda b, c: (e, 0, 0)),
        ],
        out_specs=[
            pl.BlockSpec((R_CHUNK, out_w), lambda b, c: (b * nc + c, 0)),
            pl.BlockSpec((None, R_HEADS, R_DK, R_DV), lambda b, c: (b, 0, 0, 0)),
            pl.BlockSpec((None, R_CHUNK, cmw), lambda b, c: (b, 0, 0)),
        ],
        out_shape=[
            jax.ShapeDtypeStruct((batch * seq, out_w), BF16),
            jax.ShapeDtypeStruct((batch, R_HEADS, R_DK, R_DV), F32),
            jax.ShapeDtypeStruct((batch, R_CHUNK, cmw), F32),
        ],
        scratch_shapes=[pltpu.VMEM((R_HEADS, R_DK, R_DV), F32)],
        compiler_params=_cparams(2),
        name="even_prompt",
    )(proj, cos, sin, jnp.asarray(decay), jnp.asarray(row_decay), jnp.asarray(k_decay), ws, bs4, lng3, lnb3)


def _even_sample_body(p_ref, cos_ref, sin_ref, st_in_ref, ws_ref, bs_ref, lng_ref, lnb_ref,
                      o_ref, st_ref, vn_ref, *, gamma):
    cos = cos_ref[...]
    sin = sin_ref[...]
    qk_w = R_HEADS * R_DK
    r0 = lax.broadcasted_iota(jnp.int32, (R_DK, R_DK), 0)
    r1 = lax.broadcasted_iota(jnp.int32, (R_DK, R_DK), 1)
    eye = jnp.where(r0 == r1, 1.0, 0.0).astype(BF16)
    for h in range(R_HEADS):
        q = _rope_half(p_ref[:, h * R_DK:(h + 1) * R_DK], cos, sin)
        k = _rope_half(p_ref[:, qk_w + h * R_DK:qk_w + (h + 1) * R_DK], cos, sin) * (R_DK ** -0.5)
        v = p_ref[:, 2 * qk_w + h * R_DV:2 * qk_w + (h + 1) * R_DV]
        g = p_ref[:, 3 * qk_w + h * R_DV:3 * qk_w + (h + 1) * R_DV]
        st = st_in_ref[h]
        s = jnp.sum(q * k, axis=-1, keepdims=True)
        qb = jnp.broadcast_to(q, (8, R_DK)).astype(BF16)
        cross = _dot(qb, st.astype(BF16))[0:1, :]
        o = s * v + cross * gamma[h]
        kb = jnp.broadcast_to(k, (128, R_DK)).astype(BF16)
        kcol = _dot_nt(eye, kb)
        kcol = jnp.concatenate([kcol] * (R_DV // 128), axis=-1)
        vb = v.astype(BF16).astype(F32)
        st_ref[h] = st * gamma[h] + kcol * vb
        o_ref[:, h * R_DV:(h + 1) * R_DV] = (_silu(g) * _rms(o, EPS)).astype(o_ref.dtype)

    u0 = 3 * qk_w + R_HEADS * R_DV
    cmw = CM_GROUPS * CM_GDIM
    vn = _layer_norm(_gelu_tanh(p_ref[:, u0 + cmw:u0 + 2 * cmw]), lng_ref[...], lnb_ref[...])
    vn_ref[...] = vn
    for gi in range(CM_GROUPS):
        sl = slice(gi * CM_GDIM, (gi + 1) * CM_GDIM)
        mixed = ws_ref[gi, 0:1, 0:1] * vn[:, sl] + bs_ref[gi, 0:1, :]
        u = p_ref[:, u0 + gi * CM_GDIM:u0 + (gi + 1) * CM_GDIM]
        o_ref[:, R_HEADS * R_DV + gi * CM_GDIM:R_HEADS * R_DV + (gi + 1) * CM_GDIM] = (
            _gelu_tanh(u) * mixed).astype(o_ref.dtype)


def _even_sample(proj3, cos, sin, tables, st_in, ws, bs4, lng3, lnb3, e):
    gamma = tables[4]
    nb = proj3.shape[0]
    width = proj3.shape[-1]
    cmw = CM_GROUPS * CM_GDIM
    out_w = R_HEADS * R_DV + cmw
    return pl.pallas_call(
        functools.partial(_even_sample_body, gamma=gamma),
        grid=(nb,),
        in_specs=[
            pl.BlockSpec((None, 1, width), lambda b: (b, 0, 0)),
            pl.BlockSpec((1, R_DK // 2), lambda b: (0, 0)),
            pl.BlockSpec((1, R_DK // 2), lambda b: (0, 0)),
            pl.BlockSpec((None, None, R_HEADS, R_DK, R_DV), lambda b: (e, b, 0, 0, 0)),
            pl.BlockSpec((None, CM_GROUPS, R_CHUNK, R_CHUNK), lambda b: (e, 0, 0, 0)),
            pl.BlockSpec((None, CM_GROUPS, R_CHUNK, 1), lambda b: (e, 0, 0, 0)),
            pl.BlockSpec((None, 1, cmw), lambda b: (e, 0, 0)),
            pl.BlockSpec((None, 1, cmw), lambda b: (e, 0, 0)),
        ],
        out_specs=[
            pl.BlockSpec((None, 1, out_w), lambda b: (b, 0, 0)),
            pl.BlockSpec((None, R_HEADS, R_DK, R_DV), lambda b: (b, 0, 0, 0)),
            pl.BlockSpec((None, 1, cmw), lambda b: (b, 0, 0)),
        ],
        out_shape=[
            jax.ShapeDtypeStruct((nb, 1, out_w), F32),
            jax.ShapeDtypeStruct((nb, R_HEADS, R_DK, R_DV), F32),
            jax.ShapeDtypeStruct((nb, 1, cmw), F32),
        ],
        compiler_params=_cparams(1),
        name="even_sample",
    )(proj3, cos, sin, st_in, ws, bs4, lng3, lnb3)


def _diff_lambda(lq1_ref, lk1_ref, lq2_ref, lk2_ref, lam_init):
    a = jnp.sum(lq1_ref[...] * lk1_ref[...], axis=-1, keepdims=True)
    b = jnp.sum(lq2_ref[...] * lk2_ref[...], axis=-1, keepdims=True)
    return jnp.exp(a) - jnp.exp(b) + lam_init


def _flash_body(qi_tab, ki_tab, qt_ref, k_ref, v_ref, lq1_ref, lk1_ref, lq2_ref, lk2_ref, sub_ref,
                o_ref, m_ref, acc_ref, *, lam_init):
    t = pl.program_id(2)
    qi = qi_tab[t]
    ki = ki_tab[t]
    tq = qt_ref.shape[1]
    tk = k_ref.shape[0]
    r = tq // tk

    @pl.when(ki == 0)
    def _init():
        m_ref[...] = jnp.full_like(m_ref, -jnp.inf)
        acc_ref[...] = jnp.zeros_like(acc_ref)

    def update(first, on_diag):
        cols = slice(first * tk, tq)
        w = tq - first * tk
        vt = jnp.concatenate([v_ref[...].T, jnp.ones((BF16_SUBLANES, tk), F32)], axis=0).astype(BF16)
        if on_diag:
            key = lax.broadcasted_iota(jnp.int32, (tk, w), 0)
            qry = lax.broadcasted_iota(jnp.int32, (tk, w), 1)
            keep = key <= qry
        sts = []
        for half in range(2):
            sl = slice(half * DA_DH, (half + 1) * DA_DH)
            st = _dot(k_ref[:, sl].astype(BF16), qt_ref[sl, cols])
            sts.append(jnp.where(keep, st, -jnp.inf) if on_diag else st)
        st = jnp.concatenate(sts, axis=1)
        m_old = jnp.concatenate([m_ref[0, :, cols], m_ref[1, :, cols]], axis=1)
        m_new = jnp.maximum(m_old, jnp.max(st, axis=0, keepdims=True))
        alpha = jnp.exp2(m_old - m_new)
        pt = jnp.exp2((st - m_new).astype(BF16))
        acc = alpha * jnp.concatenate([acc_ref[0, :, cols], acc_ref[1, :, cols]], axis=1) + _dot(vt, pt)
        for half in range(2):
            acc_ref[half, :, cols] = acc[:, half * w:(half + 1) * w]
            m_ref[half, :, cols] = m_new[:, half * w:(half + 1) * w]

    @pl.when(ki < qi * r)
    def _below_diag():
        update(0, False)

    for kd in range(r):
        @pl.when(ki == qi * r + kd)
        def _on_diag(kd=kd):
            update(kd, True)

    @pl.when(ki == qi * r + (r - 1))
    def _finish():
        lam = _diff_lambda(lq1_ref, lk1_ref, lq2_ref, lk2_ref, lam_init)
        a0 = acc_ref[0]
        a1 = acc_ref[1]
        ot = (a0[0:DA_VDIM] * (1.0 / a0[DA_VDIM:DA_VDIM + 1])
              - lam * (a1[0:DA_VDIM] * (1.0 / a1[DA_VDIM:DA_VDIM + 1])))
        ot = ot * lax.rsqrt(jnp.mean(ot * ot, axis=0, keepdims=True) + SUBLN_EPS)
        o_ref[...] = (ot.T * sub_ref[...] * (1.0 - lam_init)).astype(o_ref.dtype)


def _flash_prompt(q_t, k, v, lam_vecs, subln, batch, seq, lam_init, tq, tk):
    nq = seq // tq
    r = tq // tk
    qi_tab = np.concatenate([np.full((i + 1) * r, i, np.int32) for i in range(nq)])
    ki_tab = np.concatenate([np.arange((i + 1) * r, dtype=np.int32) for i in range(nq)])
    n_tri = int(qi_tab.shape[0])
    vec_spec = pl.BlockSpec((1, DA_DH), lambda b, h, t, qt, kt: (0, 0))
    grid_spec = pltpu.PrefetchScalarGridSpec(
        num_scalar_prefetch=2,
        grid=(batch, DA_HEADS, n_tri),
        in_specs=[
            pl.BlockSpec((DA_VDIM, tq), lambda b, h, t, qt, kt: (h, b * nq + qt[t])),
            pl.BlockSpec((tk, DA_VDIM), lambda b, h, t, qt, kt: (b * nq * r + kt[t], h)),
            pl.BlockSpec((tk, DA_VDIM), lambda b, h, t, qt, kt: (b * nq * r + kt[t], h)),
            vec_spec, vec_spec, vec_spec, vec_spec,
            pl.BlockSpec((1, DA_VDIM), lambda b, h, t, qt, kt: (0, 0)),
        ],
        out_specs=pl.BlockSpec((tq, DA_VDIM), lambda b, h, t, qt, kt: (b * nq + qt[t], h)),
        scratch_shapes=[
            pltpu.VMEM((2, 1, tq), F32),
            pltpu.VMEM((2, DA_VDIM + BF16_SUBLANES, tq), F32),
        ],
    )
    return pl.pallas_call(
        functools.partial(_flash_body, lam_init=lam_init),
        grid_spec=grid_spec,
        out_shape=jax.ShapeDtypeStruct((batch * seq, DA_HEADS * DA_VDIM), BF16),
        compiler_params=_cparams(3),
        name="diff_attn_prompt",
    )(jnp.asarray(qi_tab), jnp.asarray(ki_tab), q_t, k, v, *lam_vecs, subln)


def _decode_body(pt_ref, q_ref, kn_ref, vn_ref, *rest, scale, lam_init, pages_per_step):
    k_refs = rest[:pages_per_step]
    v_refs = rest[pages_per_step:2 * pages_per_step]
    (own_ref, lq1_ref, lk1_ref, lq2_ref, lk2_ref, sub_ref,
     o_ref, m_ref, l_ref, acc_ref) = rest[2 * pages_per_step:]
    step = pl.program_id(1)
    n_hh = 2 * DA_HEADS
    rows_v = v_refs[0].shape[0]

    @pl.when(step == 0)
    def _init():
        m_ref[...] = jnp.full_like(m_ref, -jnp.inf)
        l_ref[...] = jnp.zeros_like(l_ref)
        acc_ref[...] = jnp.zeros_like(acc_ref)

    q = q_ref[...]
    parity = lax.broadcasted_iota(jnp.int32, (n_hh, DA_DH), 0) & 1
    q2 = jnp.concatenate([jnp.where(parity == 0, q, 0.0), jnp.where(parity == 1, q, 0.0)], axis=1).astype(BF16)
    own = own_ref[...] > 0.0
    s_parts = []
    for kr in k_refs:
        k2 = jnp.concatenate([kr[pl.ds(0, rows_v, stride=2), :], kr[pl.ds(1, rows_v, stride=2), :]],
                             axis=1).astype(BF16)
        s_parts.append(jnp.where(own, _dot_nt(q2, k2), -jnp.inf))
    s = jnp.concatenate(s_parts, axis=1) * scale
    m_old = m_ref[...]
    m_new = jnp.maximum(m_old, jnp.max(s, axis=-1, keepdims=True))
    alpha = jnp.exp(m_old - m_new)
    p = jnp.exp(s - m_new)
    l_ref[...] = alpha * l_ref[...] + jnp.sum(p, axis=-1, keepdims=True)
    pb = p.astype(BF16)
    pv = jnp.zeros((n_hh, DA_VDIM), F32)
    for r, vr in enumerate(v_refs):
        pv += _dot(pb[:, r * rows_v:(r + 1) * rows_v], vr[...].astype(BF16))
    acc_ref[...] = alpha * acc_ref[...] + pv
    m_ref[...] = m_new

    @pl.when(step == pl.num_programs(1) - 1)
    def _finish():
        s_new = jnp.sum(q_ref[...] * kn_ref[...], axis=-1, keepdims=True) * scale
        m_old2 = m_ref[...]
        m_fin = jnp.maximum(m_old2, s_new)
        alpha2 = jnp.exp(m_old2 - m_fin)
        p_new = jnp.exp(s_new - m_fin)
        l_fin = alpha2 * l_ref[...] + p_new
        a = (alpha2 * acc_ref[...] + p_new * vn_ref[...]) / l_fin
        lam = _diff_lambda(lq1_ref, lk1_ref, lq2_ref, lk2_ref, lam_init)
        for h in range(DA_HEADS):
            o = a[2 * h:2 * h + 1, :] - lam * a[2 * h + 1:2 * h + 2, :]
            o_ref[:, h * DA_VDIM:(h + 1) * DA_VDIM] = (
                _rms(o, SUBLN_EPS) * sub_ref[...] * (1.0 - lam_init)).astype(o_ref.dtype)


def _decode_attn(q3, kn3, vn3, cache_k, cache_v, page_table, lam_vecs, subln, layer, lam_init, pages_per_step):
    nb = q3.shape[0]
    n_hh = 2 * DA_HEADS
    n_layers, n_pool, page = cache_k.shape[:3]
    n_pages = page_table.shape[1]
    n_steps = n_pages // pages_per_step
    cache_k = cache_k.reshape(n_layers, n_pool, page * n_hh, DA_DH)
    cache_v = cache_v.reshape(n_layers, n_pool, page * DA_HEADS, DA_VDIM)

    def page_spec(r, heads, dim):
        return pl.BlockSpec((None, None, page * heads, dim),
                            lambda b, s, pt: (layer, pt[b, s * pages_per_step + r], 0, 0))

    rows_v = np.arange(page * DA_HEADS)
    own = (rows_v[None, :] % DA_HEADS == np.arange(n_hh)[:, None] // 2).astype(np.float32)

    vec_spec = pl.BlockSpec((1, DA_DH), lambda b, s, pt: (0, 0))
    grid_spec = pltpu.PrefetchScalarGridSpec(
        num_scalar_prefetch=1,
        grid=(nb, n_steps),
        in_specs=[pl.BlockSpec((None, n_hh, DA_DH), lambda b, s, pt: (b, 0, 0)),
                  pl.BlockSpec((None, n_hh, DA_DH), lambda b, s, pt: (b, 0, 0)),
                  pl.BlockSpec((None, n_hh, DA_VDIM), lambda b, s, pt: (b, 0, 0))]
        + [page_spec(r, n_hh, DA_DH) for r in range(pages_per_step)]
        + [page_spec(r, DA_HEADS, DA_VDIM) for r in range(pages_per_step)]
        + [pl.BlockSpec(own.shape, lambda b, s, pt: (0, 0))]
        + [vec_spec, vec_spec, vec_spec, vec_spec,
           pl.BlockSpec((1, DA_VDIM), lambda b, s, pt: (0, 0))],
        out_specs=pl.BlockSpec((None, 1, DA_HEADS * DA_VDIM), lambda b, s, pt: (b, 0, 0)),
        scratch_shapes=[
            pltpu.VMEM((n_hh, 1), F32),
            pltpu.VMEM((n_hh, 1), F32),
            pltpu.VMEM((n_hh, DA_VDIM), F32),
        ],
    )
    return pl.pallas_call(
        functools.partial(_decode_body, scale=DA_DH ** -0.5, lam_init=lam_init, pages_per_step=pages_per_step),
        grid_spec=grid_spec,
        out_shape=jax.ShapeDtypeStruct((nb, 1, DA_HEADS * DA_VDIM), F32),
        compiler_params=_cparams(2),
        name="diff_attn_decode",
    )(page_table, q3, kn3, vn3, *([cache_k] * pages_per_step), *([cache_v] * pages_per_step),
      jnp.asarray(own), *lam_vecs, subln)


def _rope_tables(pos, d):
    inv = ROPE_THETA ** (-jnp.arange(0, d, 2, dtype=F32) / d)
    ang = pos.astype(F32)[:, None] * inv[None, :]
    return jnp.cos(ang), jnp.sin(ang)


def kernel(x_prompt, x_sample, state_ret, cache_k, cache_v, page_table, ffn_a_norm, ffn_a_w1, ffn_a_w3, ffn_a_w2, mix_norm, ffn_b_norm, ffn_b_w1, ffn_b_w3, ffn_b_w2, ev_w_in, ev_w_out, cm_ws, cm_bs, cm_ln_g, cm_ln_b, od_w_in, od_w_out, da_q_norm, da_k_norm, da_lam_q1, da_lam_k1, da_lam_q2, da_lam_k2, da_subln):
    batch, seq, d = x_prompt.shape
    nb, t_new, _ = x_sample.shape
    depth = ffn_a_norm.shape[0]
    assert t_new == 1 and seq % 1024 == 0 and nb == 8

    bm_p = 1024
    pos_p = jnp.arange(seq, dtype=jnp.int32)
    pos_s = PAST_LEN + jnp.arange(t_new, dtype=jnp.int32)
    cos_e, sin_e = _rope_tables(pos_p, R_DK)
    cos_es, sin_es = _rope_tables(pos_s, R_DK)
    cos_o, sin_o = _rope_tables(pos_p, DA_DH)
    cos_os, sin_os = _rope_tables(pos_s, DA_DH)
    c2_p = jnp.concatenate([cos_o, cos_o], axis=-1)
    s2_p = jnp.concatenate([-sin_o, sin_o], axis=-1)
    c2_s = jnp.broadcast_to(jnp.concatenate([cos_os, cos_os], axis=-1), (nb, DA_DH))
    s2_s = jnp.broadcast_to(jnp.concatenate([-sin_os, sin_os], axis=-1), (nb, DA_DH))
    tables = _retention_tables()

    as3 = lambda a: a.reshape(a.shape[0], 1, a.shape[1])
    ffn_a_norm3, ffn_b_norm3, mix_norm3 = as3(ffn_a_norm), as3(ffn_b_norm), as3(mix_norm)
    lng3, lnb3 = as3(cm_ln_g), as3(cm_ln_b)
    bs4 = cm_bs.reshape(cm_bs.shape + (1,))

    xp = x_prompt.reshape(batch * seq, d)
    xs = x_sample.reshape(nb * t_new, d)
    ret_p, ret_s, cmv_p, cmv_s = [], [], [], []
    kp_l, vp_l, ks_l, vs_l = [], [], [], []

    def ffn_pair(xp, xs, norm3, w1, w3, w2, l):
        return _ffn(xp, xs, norm3, w1, w3, w2, l, bm=bm_p, bf=512)

    for l in range(depth):
        xp, xs = ffn_pair(xp, xs, ffn_a_norm3, ffn_a_w1, ffn_a_w3, ffn_a_w2, l)
        if l % 2 == 0:
            e = l // 2
            even_in = ev_w_in.shape[-1]
            proj_p = _normproj(xp, mix_norm3, l, ev_w_in, e, bm_p, 1024)
            mix_p, st_p, vn_p = _even_prompt(proj_p, cos_e, sin_e, tables, cm_ws, bs4, lng3, lnb3, e, batch, seq)
            proj_s = _normproj(xs, mix_norm3, l, ev_w_in, e, nb, 1024)
            mix_s, st_s, vn_s = _even_sample(proj_s.reshape(nb, 1, even_in), cos_es, sin_es, tables,
                                             state_ret, cm_ws, bs4, lng3, lnb3, e)
            xp = _matres(mix_p, ev_w_out, xp, e, bm_p, 1024)
            xs = _matres(mix_s.reshape(nb, -1), ev_w_out, xs, e, nb, 2048)
            ret_p.append(st_p)
            ret_s.append(st_s)
            cmv_p.append(vn_p)
            cmv_s.append(vn_s)
        else:
            o = l // 2
            lam_init = 0.8 - 0.6 * math.exp(-0.3 * l)
            qn = da_q_norm[o].reshape(1, DA_DH)
            kn = da_k_norm[o].reshape(1, DA_DH)
            lam_vecs = [a[o].reshape(1, DA_DH) for a in (da_lam_q1, da_lam_k1, da_lam_q2, da_lam_k2)]
            subln = da_subln[o].reshape(1, DA_VDIM)
            tb = seq // bm_p
            q_s, k_s, v_s = _normproj_qkv(xs, mix_norm3, l, od_w_in, o, nb, 1024, F32, qn, kn, c2_s, s2_s, 1)
            v_rep = jnp.repeat(v_s.reshape(nb, DA_HEADS, DA_VDIM), 2, axis=1)
            att_s = _decode_attn(q_s.reshape(nb, 2 * DA_HEADS, DA_DH), k_s.reshape(nb, 2 * DA_HEADS, DA_DH), v_rep,
                                 cache_k, cache_v, page_table, lam_vecs, subln, o, lam_init, pages_per_step=8)
            xs = _matres(att_s.reshape(nb, d), od_w_out, xs, o, nb, 2048)
            xp, xs = lax.optimization_barrier((xp, xs))
            qn_scaled = qn * (DA_DH ** -0.5 * math.log2(math.e))
            q_p, k_p, v_p = _normproj_qkv(xp, mix_norm3, l, od_w_in, o, bm_p, 512, BF16, qn_scaled, kn, c2_p, s2_p, tb)
            att_p = _flash_prompt(q_p.T, k_p, v_p, lam_vecs, subln, batch, seq, lam_init, tq=2048, tk=512)
            xp = _matres(att_p, od_w_out, xp, o, bm_p, 1024)
            kp_l.append(k_p.reshape(batch, seq, 2 * DA_HEADS, DA_DH))
            vp_l.append(v_p.reshape(batch, seq, DA_HEADS, DA_VDIM))
            ks_l.append(k_s.reshape(nb, t_new, 2 * DA_HEADS, DA_DH))
            vs_l.append(v_s.reshape(nb, t_new, DA_HEADS, DA_VDIM))
        xp, xs = ffn_pair(xp, xs, ffn_b_norm3, ffn_b_w1, ffn_b_w3, ffn_b_w2, l)
        if l % 2 == 1 and l + 1 < depth:
            xp, kp_l[-1], vp_l[-1] = lax.optimization_barrier((xp, kp_l[-1], vp_l[-1]))

    return (xp.reshape(batch, seq, d), xs.reshape(nb, t_new, d),
            jnp.stack(ret_p), jnp.stack(ret_s), jnp.stack(cmv_p), jnp.stack(cmv_s),
            jnp.stack(kp_l), jnp.stack(vp_l), jnp.stack(ks_l), jnp.stack(vs_l))
```

```python
import functools
import math

import numpy as np
import jax
import jax.numpy as jnp
from jax import lax
from jax.experimental import pallas as pl
from jax.experimental.pallas import tpu as pltpu

F32 = jnp.float32
BF16 = jnp.bfloat16

PAST_LEN = 16384
R_HEADS = 4
R_DK = 256
R_DV = 256
R_CHUNK = 128
CM_GROUPS = 4
CM_GDIM = 256
DA_HEADS = 8
DA_DH = 128
DA_VDIM = 256
LOG2_DA_DH = 7
LOG2_DA_VDIM = 8
ROPE_THETA = 10000.0
EPS = 1e-6
SUBLN_EPS = 1e-5
LN_EPS = 1e-5

BF16_SUBLANES = 16
VMEM_LIMIT_BYTES = 56 * 1024 * 1024
FFN_VMEM_LIMIT_BYTES = 60000 * 1024


def _cparams(n_axes, vmem_limit_bytes=VMEM_LIMIT_BYTES):
    return pltpu.CompilerParams(
        dimension_semantics=("arbitrary",) * n_axes,
        vmem_limit_bytes=vmem_limit_bytes,
    )


def _silu(a):
    return a / (1.0 + jnp.exp(-a))


def _gelu_tanh(x):
    c = math.sqrt(2.0 / math.pi)
    return x * (0.5 * (1.0 + jnp.tanh(c * (x + 0.044715 * (x * x * x)))))


def _rms(x, eps):
    return x * lax.rsqrt(jnp.mean(x * x, axis=-1, keepdims=True) + eps)


def _dot(a, b):
    return jnp.dot(a, b, preferred_element_type=F32)


def _dot_nt(a, b):
    return lax.dot_general(a, b, (((1,), (1,)), ((), ())), preferred_element_type=F32)


def _ffn_body(x_ref, xs_ref, g_ref, w1_ref, w3_ref, w2_ref, o_ref, os_ref, xn_ref, *, n_split):
    i = pl.program_id(0)
    j = pl.program_id(1)
    bm = x_ref.shape[0]
    ms = xs_ref.shape[0]
    tail = xn_ref.shape[0] - bm

    @pl.when(j == 0)
    def _init():
        x = x_ref[...]
        xn_ref[0:bm, :] = (_rms(x, EPS) * g_ref[...]).astype(BF16)
        o_ref[...] = x

    @pl.when((i == 0) & (j == 0))
    def _init_sample():
        xs = xs_ref[...]
        xsn = jnp.concatenate([_rms(xs, EPS) * g_ref[...], jnp.zeros((tail - ms, xs.shape[1]), F32)], axis=0)
        xn_ref[bm:bm + tail, :] = xsn.astype(BF16)
        os_ref[...] = xs

    xn = xn_ref[...]
    a = _dot(xn, w1_ref[...].astype(BF16))
    b = _dot(xn, w3_ref[...].astype(BF16))
    h = ((_silu(a) * b) * 0.5).astype(BF16)
    d = o_ref.shape[-1]
    w = d // n_split
    for n in range(n_split):
        sl = slice(n * w, (n + 1) * w)
        y = _dot(h, w2_ref[:, sl].astype(BF16))
        o_ref[:, sl] += y[0:bm]
        os_ref[:, sl] += jnp.where(i == 0, y[bm:bm + ms], 0.0)


def _ffn(x, xs, norm3, w1, w3, w2, layer, bm, bf):
    m, d = x.shape
    ms = xs.shape[0]
    ff = w1.shape[-1]
    return pl.pallas_call(
        functools.partial(_ffn_body, n_split=4),
        grid=(m // bm, ff // bf),
        in_specs=[
            pl.BlockSpec((bm, d), lambda i, j: (i, 0), pipeline_mode=pl.Buffered(1)),
            pl.BlockSpec((ms, d), lambda i, j: (0, 0)),
            pl.BlockSpec((None, 1, d), lambda i, j: (layer, 0, 0)),
            pl.BlockSpec((None, d, bf), lambda i, j: (layer, 0, j)),
            pl.BlockSpec((None, d, bf), lambda i, j: (layer, 0, j)),
            pl.BlockSpec((None, bf, d), lambda i, j: (layer, j, 0)),
        ],
        out_specs=[
            pl.BlockSpec((bm, d), lambda i, j: (i, 0)),
            pl.BlockSpec((ms, d), lambda i, j: (0, 0)),
        ],
        out_shape=[jax.ShapeDtypeStruct((m, d), F32), jax.ShapeDtypeStruct((ms, d), F32)],
        scratch_shapes=[pltpu.VMEM((bm + BF16_SUBLANES, d), BF16)],
        compiler_params=_cparams(2, FFN_VMEM_LIMIT_BYTES),
        name="ffn",
    )(x, xs, norm3, w1, w3, w2)


def _normproj_body(x_ref, g_ref, w_ref, o_ref, xn_ref):
    @pl.when(pl.program_id(1) == 0)
    def _init():
        xn_ref[...] = (_rms(x_ref[...], EPS) * g_ref[...]).astype(BF16)

    o_ref[...] = _dot(xn_ref[...], w_ref[...].astype(BF16)).astype(o_ref.dtype)


def _normproj(x, norm3, norm_layer, w, layer, bm, bn):
    m, d = x.shape
    ncols = w.shape[-1]
    return pl.pallas_call(
        _normproj_body,
        grid=(m // bm, ncols // bn),
        in_specs=[
            pl.BlockSpec((bm, d), lambda i, j: (i, 0)),
            pl.BlockSpec((None, 1, d), lambda i, j: (norm_layer, 0, 0)),
            pl.BlockSpec((None, d, bn), lambda i, j: (layer, 0, j)),
        ],
        out_specs=pl.BlockSpec((bm, bn), lambda i, j: (i, j)),
        out_shape=jax.ShapeDtypeStruct((m, ncols), F32),
        scratch_shapes=[pltpu.VMEM((bm, d), BF16)],
        compiler_params=_cparams(2),
        name="normproj",
    )(x, norm3, w)


def _normproj_qkv_body(x_ref, g_ref, w_ref, qn_ref, kn_ref, c_ref, s_ref, q_ref, k_ref, v_ref, xn_ref, *, nj):
    j = pl.program_id(1)

    @pl.when(j == 0)
    def _init():
        xn_ref[...] = (_rms(x_ref[...], EPS) * g_ref[...]).astype(BF16)

    y = _dot(xn_ref[...], w_ref[...].astype(BF16))

    def norm_rope(hn_ref, o_ref):
        hn = hn_ref[...]
        c = c_ref[...]
        s = s_ref[...]
        for t in range(y.shape[-1] // DA_DH):
            sl = slice(t * DA_DH, (t + 1) * DA_DH)
            yn = _rms(y[:, sl], EPS) * hn
            o_ref[:, sl] = (yn * c + pltpu.roll(yn, DA_DH // 2, axis=1) * s).astype(o_ref.dtype)

    @pl.when(j < nj)
    def _q():
        norm_rope(qn_ref, q_ref)

    @pl.when((j >= nj) & (j < 2 * nj))
    def _k():
        norm_rope(kn_ref, k_ref)

    @pl.when(j >= 2 * nj)
    def _v():
        v_ref[...] = y


def _normproj_qkv(x, norm3, norm_layer, w, layer, bm, bn, q_dtype, qn, kn, c2, s2, table_blocks):
    m, d = x.shape
    nj = d // bn
    clip = lambda j, lo: jnp.clip(j - lo, 0, nj - 1)
    return pl.pallas_call(
        functools.partial(_normproj_qkv_body, nj=nj),
        grid=(m // bm, 3 * nj),
        in_specs=[
            pl.BlockSpec((bm, d), lambda i, j: (i, 0)),
            pl.BlockSpec((None, 1, d), lambda i, j: (norm_layer, 0, 0)),
            pl.BlockSpec((None, d, bn), lambda i, j: (layer, 0, j)),
            pl.BlockSpec((1, DA_DH), lambda i, j: (0, 0)),
            pl.BlockSpec((1, DA_DH), lambda i, j: (0, 0)),
            pl.BlockSpec((bm, DA_DH), lambda i, j: (i % table_blocks, 0)),
            pl.BlockSpec((bm, DA_DH), lambda i, j: (i % table_blocks, 0)),
        ],
        out_specs=[
            pl.BlockSpec((bm, bn), lambda i, j: (i, clip(j, 0))),
            pl.BlockSpec((bm, bn), lambda i, j: (i, clip(j, nj))),
            pl.BlockSpec((bm, bn), lambda i, j: (i, clip(j, 2 * nj))),
        ],
        out_shape=[
            jax.ShapeDtypeStruct((m, d), q_dtype),
            jax.ShapeDtypeStruct((m, d), F32),
            jax.ShapeDtypeStruct((m, d), F32),
        ],
        scratch_shapes=[pltpu.VMEM((bm, d), BF16)],
        compiler_params=_cparams(2),
        name="normproj_qkv",
    )(x, norm3, w, qn, kn, c2, s2)


def _matres_body(a_ref, w_ref, r_ref, o_ref):
    o_ref[...] = r_ref[...] + _dot(a_ref[...].astype(BF16), w_ref[...].astype(BF16))


def _matres(a, w, res, layer, bm, bn):
    m, k = a.shape
    n = w.shape[-1]
    return pl.pallas_call(
        _matres_body,
        grid=(n // bn, m // bm),
        in_specs=[
            pl.BlockSpec((bm, k), lambda j, i: (i, 0)),
            pl.BlockSpec((None, k, bn), lambda j, i: (layer, 0, j)),
            pl.BlockSpec((bm, bn), lambda j, i: (i, j)),
        ],
        out_specs=pl.BlockSpec((bm, bn), lambda j, i: (i, j)),
        out_shape=jax.ShapeDtypeStruct((m, n), F32),
        compiler_params=_cparams(2),
        name="matres",
    )(a, w, res)


def _retention_tables():
    expo = -5.0 - 7.0 * np.arange(R_HEADS, dtype=np.float64) / max(R_HEADS - 1, 1)
    log_g = np.log1p(-np.exp2(expo))
    idx = np.arange(R_CHUNK, dtype=np.float64)
    diff = idx[:, None] - idx[None, :]
    decay = np.where(diff >= 0, np.exp(log_g[:, None, None] * np.maximum(diff, 0.0)), 0.0)
    row_decay = np.exp(log_g[:, None] * (idx + 1.0))[:, :, None]
    k_decay = np.exp(log_g[:, None] * (R_CHUNK - 1.0 - idx))[:, :, None]
    state_decay = tuple(float(v) for v in np.exp(log_g * R_CHUNK))
    gamma = tuple(float(v) for v in np.exp(log_g))
    return (decay.astype(np.float32), row_decay.astype(np.float32), k_decay.astype(np.float32),
            state_decay, gamma)


def _rope_half(x, cos, sin):
    half = x.shape[-1] // 2
    x1 = x[:, :half]
    x2 = x[:, half:]
    return jnp.concatenate([x1 * cos - x2 * sin, x2 * cos + x1 * sin], axis=-1)


def _layer_norm(x, g, b):
    xc = x - jnp.mean(x, axis=-1, keepdims=True)
    return xc * lax.rsqrt(jnp.mean(xc * xc, axis=-1, keepdims=True) + LN_EPS) * g + b


def _even_prompt_body(p_ref, cos_ref, sin_ref, dec_ref, rd_ref, kd_ref, ws_ref, bs_ref, lng_ref, lnb_ref,
                      o_ref, st_ref, vn_ref, state, *, state_decay):
    c = pl.program_id(1)
    last = pl.num_programs(1) - 1

    @pl.when(c == 0)
    def _zero():
        state[...] = jnp.zeros_like(state)

    cos = cos_ref[...]
    sin = sin_ref[...]
    qk_w = R_HEADS * R_DK
    for h in range(R_HEADS):
        q = _rope_half(p_ref[:, h * R_DK:(h + 1) * R_DK], cos, sin)
        k = _rope_half(p_ref[:, qk_w + h * R_DK:qk_w + (h + 1) * R_DK], cos, sin) * (R_DK ** -0.5)
        v = p_ref[:, 2 * qk_w + h * R_DV:2 * qk_w + (h + 1) * R_DV].astype(BF16)
        g = p_ref[:, 3 * qk_w + h * R_DV:3 * qk_w + (h + 1) * R_DV]
        qb = q.astype(BF16)
        s = _dot_nt(qb, k.astype(BF16)) * dec_ref[h]
        st = state[h]
        o = _dot(s.astype(BF16), v) + _dot(qb, st.astype(BF16)) * rd_ref[h]
        kd_t = (k * kd_ref[h]).T.astype(BF16)
        state[h] = st * state_decay[h] + _dot(kd_t, v)
        o_ref[:, h * R_DV:(h + 1) * R_DV] = (_silu(g) * _rms(o, EPS)).astype(o_ref.dtype)

    u0 = 3 * qk_w + R_HEADS * R_DV
    cmw = CM_GROUPS * CM_GDIM
    vn = _layer_norm(_gelu_tanh(p_ref[:, u0 + cmw:u0 + 2 * cmw]), lng_ref[...], lnb_ref[...])
    row = lax.broadcasted_iota(jnp.int32, (R_CHUNK, R_CHUNK), 0)
    col = lax.broadcasted_iota(jnp.int32, (R_CHUNK, R_CHUNK), 1)
    for gi in range(CM_GROUPS):
        sl = slice(gi * CM_GDIM, (gi + 1) * CM_GDIM)
        w = jnp.where(row >= col, ws_ref[gi], 0.0).astype(BF16)
        mixed = _dot(w, vn[:, sl].astype(BF16)) + bs_ref[gi]
        u = p_ref[:, u0 + gi * CM_GDIM:u0 + (gi + 1) * CM_GDIM]
        o_ref[:, R_HEADS * R_DV + gi * CM_GDIM:R_HEADS * R_DV + (gi + 1) * CM_GDIM] = (
            _gelu_tanh(u) * mixed).astype(o_ref.dtype)

    @pl.when(c == last)
    def _emit():
        st_ref[...] = state[...]
        vn_ref[...] = vn


def _even_prompt(proj, cos, sin, tables, ws, bs4, lng3, lnb3, e, batch, seq):
    decay, row_decay, k_decay, state_decay, _ = tables
    nc = seq // R_CHUNK
    width = proj.shape[-1]
    cmw = CM_GROUPS * CM_GDIM
    out_w = R_HEADS * R_DV + cmw
    const3 = lambda b, c: (0, 0, 0)
    return pl.pallas_call(
        functools.partial(_even_prompt_body, state_decay=state_decay),
        grid=(batch, nc),
        in_specs=[
            pl.BlockSpec((R_CHUNK, width), lambda b, c: (b * nc + c, 0)),
            pl.BlockSpec((R_CHUNK, R_DK // 2), lambda b, c: (c, 0)),
            pl.BlockSpec((R_CHUNK, R_DK // 2), lambda b, c: (c, 0)),
            pl.BlockSpec(decay.shape, const3),
            pl.BlockSpec(row_decay.shape, const3),
            pl.BlockSpec(k_decay.shape, const3),
            pl.BlockSpec((None, CM_GROUPS, R_CHUNK, R_CHUNK), lambda b, c: (e, 0, 0, 0)),
            pl.BlockSpec((None, CM_GROUPS, R_CHUNK, 1), lambda b, c: (e, 0, 0, 0)),
            pl.BlockSpec((None, 1, cmw), lambda b, c: (e, 0, 0)),
            pl.BlockSpec((None, 1, cmw), lambda b, c: (e, 0, 0)),
        ],
        out_specs=[
            pl.BlockSpec((R_CHUNK, out_w), lambda b, c: (b * nc + c, 0)),
            pl.BlockSpec((None, R_HEADS, R_DK, R_DV), lambda b, c: (b, 0, 0, 0)),
            pl.BlockSpec((None, R_CHUNK, cmw), lambda b, c: (b, 0, 0)),
        ],
        out_shape=[
            jax.ShapeDtypeStruct((batch * seq, out_w), BF16),
            jax.ShapeDtypeStruct((batch, R_HEADS, R_DK, R_DV), F32),
            jax.ShapeDtypeStruct((batch, R_CHUNK, cmw), F32),
        ],
        scratch_shapes=[pltpu.VMEM((R_HEADS, R_DK, R_DV), F32)],
        compiler_params=_cparams(2),
        name="even_prompt",
    )(proj, cos, sin, jnp.asarray(decay), jnp.asarray(row_decay), jnp.asarray(k_decay), ws, bs4, lng3, lnb3)


def _even_sample_body(p_ref, cos_ref, sin_ref, st_in_ref, ws_ref, bs_ref, lng_ref, lnb_ref,
                      o_ref, st_ref, vn_ref, *, gamma):
    cos = cos_ref[...]
    sin = sin_ref[...]
    qk_w = R_HEADS * R_DK
    r0 = lax.broadcasted_iota(jnp.int32, (R_DK, R_DK), 0)
    r1 = lax.broadcasted_iota(jnp.int32, (R_DK, R_DK), 1)
    eye = jnp.where(r0 == r1, 1.0, 0.0).astype(BF16)
    for h in range(R_HEADS):
        q = _rope_half(p_ref[:, h * R_DK:(h + 1) * R_DK], cos, sin)
        k = _rope_half(p_ref[:, qk_w + h * R_DK:qk_w + (h + 1) * R_DK], cos, sin) * (R_DK ** -0.5)
        v = p_ref[:, 2 * qk_w + h * R_DV:2 * qk_w + (h + 1) * R_DV]
        g = p_ref[:, 3 * qk_w + h * R_DV:3 * qk_w + (h + 1) * R_DV]
        st = st_in_ref[h]
        s = jnp.sum(q * k, axis=-1, keepdims=True)
        qb = jnp.broadcast_to(q, (8, R_DK)).astype(BF16)
        cross = _dot(qb, st.astype(BF16))[0:1, :]
        o = s * v + cross * gamma[h]
        kb = jnp.broadcast_to(k, (128, R_DK)).astype(BF16)
        kcol = _dot_nt(eye, kb)
        kcol = jnp.concatenate([kcol] * (R_DV // 128), axis=-1)
        vb = v.astype(BF16).astype(F32)
        st_ref[h] = st * gamma[h] + kcol * vb
        o_ref[:, h * R_DV:(h + 1) * R_DV] = (_silu(g) * _rms(o, EPS)).astype(o_ref.dtype)

    u0 = 3 * qk_w + R_HEADS * R_DV
    cmw = CM_GROUPS * CM_GDIM
    vn = _layer_norm(_gelu_tanh(p_ref[:, u0 + cmw:u0 + 2 * cmw]), lng_ref[...], lnb_ref[...])
    vn_ref[...] = vn
    for gi in range(CM_GROUPS):
        sl = slice(gi * CM_GDIM, (gi + 1) * CM_GDIM)
        mixed = ws_ref[gi, 0:1, 0:1] * vn[:, sl] + bs_ref[gi, 0:1, :]
        u = p_ref[:, u0 + gi * CM_GDIM:u0 + (gi + 1) * CM_GDIM]
        o_ref[:, R_HEADS * R_DV + gi * CM_GDIM:R_HEADS * R_DV + (gi + 1) * CM_GDIM] = (
            _gelu_tanh(u) * mixed).astype(o_ref.dtype)


def _even_sample(proj3, cos, sin, tables, st_in, ws, bs4, lng3, lnb3, e):
    gamma = tables[4]
    nb = proj3.shape[0]
    width = proj3.shape[-1]
    cmw = CM_GROUPS * CM_GDIM
    out_w = R_HEADS * R_DV + cmw
    return pl.pallas_call(
        functools.partial(_even_sample_body, gamma=gamma),
        grid=(nb,),
        in_specs=[
            pl.BlockSpec((None, 1, width), lambda b: (b, 0, 0)),
            pl.BlockSpec((1, R_DK // 2), lambda b: (0, 0)),
            pl.BlockSpec((1, R_DK // 2), lambda b: (0, 0)),
            pl.BlockSpec((None, None, R_HEADS, R_DK, R_DV), lambda b: (e, b, 0, 0, 0)),
            pl.BlockSpec((None, CM_GROUPS, R_CHUNK, R_CHUNK), lambda b: (e, 0, 0, 0)),
            pl.BlockSpec((None, CM_GROUPS, R_CHUNK, 1), lambda b: (e, 0, 0, 0)),
            pl.BlockSpec((None, 1, cmw), lambda b: (e, 0, 0)),
            pl.BlockSpec((None, 1, cmw), lambda b: (e, 0, 0)),
        ],
        out_specs=[
            pl.BlockSpec((None, 1, out_w), lambda b: (b, 0, 0)),
            pl.BlockSpec((None, R_HEADS, R_DK, R_DV), lambda b: (b, 0, 0, 0)),
            pl.BlockSpec((None, 1, cmw), lambda b: (b, 0, 0)),
        ],
        out_shape=[
            jax.ShapeDtypeStruct((nb, 1, out_w), F32),
            jax.ShapeDtypeStruct((nb, R_HEADS, R_DK, R_DV), F32),
            jax.ShapeDtypeStruct((nb, 1, cmw), F32),
        ],
        compiler_params=_cparams(1),
        name="even_sample",
    )(proj3, cos, sin, st_in, ws, bs4, lng3, lnb3)


def _diff_lambda(lq1_ref, lk1_ref, lq2_ref, lk2_ref, lam_init):
    a = jnp.sum(lq1_ref[...] * lk1_ref[...], axis=-1, keepdims=True)
    b = jnp.sum(lq2_ref[...] * lk2_ref[...], axis=-1, keepdims=True)
    return jnp.exp(a) - jnp.exp(b) + lam_init


def _flash_body(qi_tab, ki_tab, qt_ref, k_ref, v_ref, lq1_ref, lk1_ref, lq2_ref, lk2_ref, sub_ref,
                o_ref, m_ref, acc_ref, *, lam_init):
    t = pl.program_id(2)
    qi = qi_tab[t]
    ki = ki_tab[t]
    tq = qt_ref.shape[1]
    tk = k_ref.shape[0]
    r = tq // tk

    @pl.when(ki == 0)
    def _init():
        m_ref[...] = jnp.full_like(m_ref, -jnp.inf)
        acc_ref[...] = jnp.zeros_like(acc_ref)

    def update(first, on_diag):
        cols = slice(first * tk, tq)
        w = tq - first * tk
        vt = jnp.concatenate([v_ref[...].T, jnp.ones((BF16_SUBLANES, tk), F32)], axis=0).astype(BF16)
        if on_diag:
            key = lax.broadcasted_iota(jnp.int32, (tk, w), 0)
            qry = lax.broadcasted_iota(jnp.int32, (tk, w), 1)
            keep = key <= qry
        sts = []
        for half in range(2):
            sl = slice(half * DA_DH, (half + 1) * DA_DH)
            st = _dot(k_ref[:, sl].astype(BF16), qt_ref[sl, cols])
            sts.append(jnp.where(keep, st, -jnp.inf) if on_diag else st)
        st = jnp.concatenate(sts, axis=1)
        m_old = jnp.concatenate([m_ref[0, :, cols], m_ref[1, :, cols]], axis=1)
        m_new = jnp.maximum(m_old, jnp.max(st, axis=0, keepdims=True))
        alpha = jnp.exp2(m_old - m_new)
        pt = jnp.exp2((st - m_new).astype(BF16))
        acc = alpha * jnp.concatenate([acc_ref[0, :, cols], acc_ref[1, :, cols]], axis=1) + _dot(vt, pt)
        for half in range(2):
            acc_ref[half, :, cols] = acc[:, half * w:(half + 1) * w]
            m_ref[half, :, cols] = m_new[:, half * w:(half + 1) * w]

    @pl.when(ki < qi * r)
    def _below_diag():
        update(0, False)

    for kd in range(r):
        @pl.when(ki == qi * r + kd)
        def _on_diag(kd=kd):
            update(kd, True)

    @pl.when(ki == qi * r + (r - 1))
    def _finish():
        lam = _diff_lambda(lq1_ref, lk1_ref, lq2_ref, lk2_ref, lam_init)
        a0 = acc_ref[0]
        a1 = acc_ref[1]
        ot = (a0[0:DA_VDIM] * (1.0 / a0[DA_VDIM:DA_VDIM + 1])
              - lam * (a1[0:DA_VDIM] * (1.0 / a1[DA_VDIM:DA_VDIM + 1])))
        ot = ot * lax.rsqrt(jnp.mean(ot * ot, axis=0, keepdims=True) + SUBLN_EPS)
        o_ref[...] = (ot.T * sub_ref[...] * (1.0 - lam_init)).astype(o_ref.dtype)


def _flash_prompt(q_t, k, v, lam_vecs, subln, batch, seq, lam_init, tq, tk):
    nq = seq // tq
    r = tq // tk
    qi_tab = np.concatenate([np.full((i + 1) * r, i, np.int32) for i in range(nq)])
    ki_tab = np.concatenate([np.arange((i + 1) * r, dtype=np.int32) for i in range(nq)])
    n_tri = int(qi_tab.shape[0])
    vec_spec = pl.BlockSpec((1, DA_DH), lambda b, h, t, qt, kt: (0, 0))
    grid_spec = pltpu.PrefetchScalarGridSpec(
        num_scalar_prefetch=2,
        grid=(batch, DA_HEADS, n_tri),
        in_specs=[
            pl.BlockSpec((DA_VDIM, tq), lambda b, h, t, qt, kt: (h, b * nq + qt[t])),
            pl.BlockSpec((tk, DA_VDIM), lambda b, h, t, qt, kt: (b * nq * r + kt[t], h)),
            pl.BlockSpec((tk, DA_VDIM), lambda b, h, t, qt, kt: (b * nq * r + kt[t], h)),
            vec_spec, vec_spec, vec_spec, vec_spec,
            pl.BlockSpec((1, DA_VDIM), lambda b, h, t, qt, kt: (0, 0)),
        ],
        out_specs=pl.BlockSpec((tq, DA_VDIM), lambda b, h, t, qt, kt: (b * nq + qt[t], h)),
        scratch_shapes=[
            pltpu.VMEM((2, 1, tq), F32),
            pltpu.VMEM((2, DA_VDIM + BF16_SUBLANES, tq), F32),
        ],
    )
    return pl.pallas_call(
        functools.partial(_flash_body, lam_init=lam_init),
        grid_spec=grid_spec,
        out_shape=jax.ShapeDtypeStruct((batch * seq, DA_HEADS * DA_VDIM), BF16),
        compiler_params=_cparams(3),
        name="diff_attn_prompt",
    )(jnp.asarray(qi_tab), jnp.asarray(ki_tab), q_t, k, v, *lam_vecs, subln)


def _decode_body(pt_ref, q_ref, kn_ref, vn_ref, *rest, scale, lam_init, pages_per_step):
    k_refs = rest[:pages_per_step]
    v_refs = rest[pages_per_step:2 * pages_per_step]
    (own_ref, lq1_ref, lk1_ref, lq2_ref, lk2_ref, sub_ref,
     o_ref, m_ref, l_ref, acc_ref) = rest[2 * pages_per_step:]
    step = pl.program_id(1)
    n_hh = 2 * DA_HEADS
    rows_v = v_refs[0].shape[0]

    @pl.when(step == 0)
    def _init():
        m_ref[...] = jnp.full_like(m_ref, -jnp.inf)
        l_ref[...] = jnp.zeros_like(l_ref)
        acc_ref[...] = jnp.zeros_like(acc_ref)

    q = q_ref[...]
    parity = lax.broadcasted_iota(jnp.int32, (n_hh, DA_DH), 0) & 1
    q2 = jnp.concatenate([jnp.where(parity == 0, q, 0.0), jnp.where(parity == 1, q, 0.0)], axis=1).astype(BF16)
    own = own_ref[...] > 0.0
    s_parts = []
    for kr in k_refs:
        k2 = jnp.concatenate([kr[pl.ds(0, rows_v, stride=2), :], kr[pl.ds(1, rows_v, stride=2), :]],
                             axis=1).astype(BF16)
        s_parts.append(jnp.where(own, _dot_nt(q2, k2), -jnp.inf))
    s = jnp.concatenate(s_parts, axis=1) * scale
    m_old = m_ref[...]
    m_new = jnp.maximum(m_old, jnp.max(s, axis=-1, keepdims=True))
    alpha = jnp.exp(m_old - m_new)
    p = jnp.exp(s - m_new)
    l_ref[...] = alpha * l_ref[...] + jnp.sum(p, axis=-1, keepdims=True)
    pb = p.astype(BF16)
    pv = jnp.zeros((n_hh, DA_VDIM), F32)
    for r, vr in enumerate(v_refs):
        pv += _dot(pb[:, r * rows_v:(r + 1) * rows_v], vr[...].astype(BF16))
    acc_ref[...] = alpha * acc_ref[...] + pv
    m_ref[...] = m_new

    @pl.when(step == pl.num_programs(1) - 1)
    def _finish():
        s_new = jnp.sum(q_ref[...] * kn_ref[...], axis=-1, keepdims=True) * scale
        m_old2 = m_ref[...]
        m_fin = jnp.maximum(m_old2, s_new)
        alpha2 = jnp.exp(m_old2 - m_fin)
        p_new = jnp.exp(s_new - m_fin)
        l_fin = alpha2 * l_ref[...] + p_new
        a = (alpha2 * acc_ref[...] + p_new * vn_ref[...]) / l_fin
        lam = _diff_lambda(lq1_ref, lk1_ref, lq2_ref, lk2_ref, lam_init)
        for h in range(DA_HEADS):
            o = a[2 * h:2 * h + 1, :] - lam * a[2 * h + 1:2 * h + 2, :]
            o_ref[:, h * DA_VDIM:(h + 1) * DA_VDIM] = (
                _rms(o, SUBLN_EPS) * sub_ref[...] * (1.0 - lam_init)).astype(o_ref.dtype)


def _decode_attn(q3, kn3, vn3, cache_k, cache_v, page_table, lam_vecs, subln, layer, lam_init, pages_per_step):
    nb = q3.shape[0]
    n_hh = 2 * DA_HEADS
    n_layers, n_pool, page = cache_k.shape[:3]
    n_pages = page_table.shape[1]
    n_steps = n_pages // pages_per_step
    cache_k = cache_k.reshape(n_layers, n_pool, page * n_hh, DA_DH)
    cache_v = cache_v.reshape(n_layers, n_pool, page * DA_HEADS, DA_VDIM)

    def page_spec(r, heads, dim):
        return pl.BlockSpec((None, None, page * heads, dim),
                            lambda b, s, pt: (layer, pt[b, s * pages_per_step + r], 0, 0))

    rows_v = np.arange(page * DA_HEADS)
    own = (rows_v[None, :] % DA_HEADS == np.arange(n_hh)[:, None] // 2).astype(np.float32)

    vec_spec = pl.BlockSpec((1, DA_DH), lambda b, s, pt: (0, 0))
    grid_spec = pltpu.PrefetchScalarGridSpec(
        num_scalar_prefetch=1,
        grid=(nb, n_steps),
        in_specs=[pl.BlockSpec((None, n_hh, DA_DH), lambda b, s, pt: (b, 0, 0)),
                  pl.BlockSpec((None, n_hh, DA_DH), lambda b, s, pt: (b, 0, 0)),
                  pl.BlockSpec((None, n_hh, DA_VDIM), lambda b, s, pt: (b, 0, 0))]
        + [page_spec(r, n_hh, DA_DH) for r in range(pages_per_step)]
        + [page_spec(r, DA_HEADS, DA_VDIM) for r in range(pages_per_step)]
        + [pl.BlockSpec(own.shape, lambda b, s, pt: (0, 0))]
        + [vec_spec, vec_spec, vec_spec, vec_spec,
           pl.BlockSpec((1, DA_VDIM), lambda b, s, pt: (0, 0))],
        out_specs=pl.BlockSpec((None, 1, DA_HEADS * DA_VDIM), lambda b, s, pt: (b, 0, 0)),
        scratch_shapes=[
            pltpu.VMEM((n_hh, 1), F32),
            pltpu.VMEM((n_hh, 1), F32),
            pltpu.VMEM((n_hh, DA_VDIM), F32),
        ],
    )
    return pl.pallas_call(
        functools.partial(_decode_body, scale=DA_DH ** -0.5, lam_init=lam_init, pages_per_step=pages_per_step),
        grid_spec=grid_spec,
        out_shape=jax.ShapeDtypeStruct((nb, 1, DA_HEADS * DA_VDIM), F32),
        compiler_params=_cparams(2),
        name="diff_attn_decode",
    )(page_table, q3, kn3, vn3, *([cache_k] * pages_per_step), *([cache_v] * pages_per_step),
      jnp.asarray(own), *lam_vecs, subln)


def _rope_tables(pos, d):
    inv = ROPE_THETA ** (-jnp.arange(0, d, 2, dtype=F32) / d)
    ang = pos.astype(F32)[:, None] * inv[None, :]
    return jnp.cos(ang), jnp.sin(ang)


def kernel(x_prompt, x_sample, state_ret, cache_k, cache_v, page_table, ffn_a_norm, ffn_a_w1, ffn_a_w3, ffn_a_w2, mix_norm, ffn_b_norm, ffn_b_w1, ffn_b_w3, ffn_b_w2, ev_w_in, ev_w_out, cm_ws, cm_bs, cm_ln_g, cm_ln_b, od_w_in, od_w_out, da_q_norm, da_k_norm, da_lam_q1, da_lam_k1, da_lam_q2, da_lam_k2, da_subln):
    batch, seq, d = x_prompt.shape
    nb, t_new, _ = x_sample.shape
    depth = ffn_a_norm.shape[0]
    assert t_new == 1 and seq % 2048 == 0 and nb == 8

    bm_p = 1024
    pos_p = jnp.arange(seq, dtype=jnp.int32)
    pos_s = PAST_LEN + jnp.arange(t_new, dtype=jnp.int32)
    cos_e, sin_e = _rope_tables(pos_p, R_DK)
    cos_es, sin_es = _rope_tables(pos_s, R_DK)
    cos_o, sin_o = _rope_tables(pos_p, DA_DH)
    cos_os, sin_os = _rope_tables(pos_s, DA_DH)
    c2_p = jnp.concatenate([cos_o, cos_o], axis=-1)
    s2_p = jnp.concatenate([-sin_o, sin_o], axis=-1)
    c2_s = jnp.broadcast_to(jnp.concatenate([cos_os, cos_os], axis=-1), (nb, DA_DH))
    s2_s = jnp.broadcast_to(jnp.concatenate([-sin_os, sin_os], axis=-1), (nb, DA_DH))
    tables = _retention_tables()

    as3 = lambda a: a.reshape(a.shape[0], 1, a.shape[1])
    ffn_a_norm3, ffn_b_norm3, mix_norm3 = as3(ffn_a_norm), as3(ffn_b_norm), as3(mix_norm)
    lng3, lnb3 = as3(cm_ln_g), as3(cm_ln_b)
    bs4 = cm_bs.reshape(cm_bs.shape + (1,))

    xp = x_prompt.reshape(batch * seq, d)
    xs = x_sample.reshape(nb * t_new, d)
    ret_p, ret_s, cmv_p, cmv_s = [], [], [], []
    kp_l, vp_l, ks_l, vs_l = [], [], [], []

    def ffn_pair(xp, xs, norm3, w1, w3, w2, l):
        return _ffn(xp, xs, norm3, w1, w3, w2, l, bm=bm_p, bf=512)

    for l in range(depth):
        xp, xs = ffn_pair(xp, xs, ffn_a_norm3, ffn_a_w1, ffn_a_w3, ffn_a_w2, l)
        if l % 2 == 0:
            e = l // 2
            even_in = ev_w_in.shape[-1]
            proj_p = _normproj(xp, mix_norm3, l, ev_w_in, e, bm_p, 1024)
            mix_p, st_p, vn_p = _even_prompt(proj_p, cos_e, sin_e, tables, cm_ws, bs4, lng3, lnb3, e, batch, seq)
            proj_s = _normproj(xs, mix_norm3, l, ev_w_in, e, nb, 1024)
            mix_s, st_s, vn_s = _even_sample(proj_s.reshape(nb, 1, even_in), cos_es, sin_es, tables,
                                             state_ret, cm_ws, bs4, lng3, lnb3, e)
            xp = _matres(mix_p, ev_w_out, xp, e, bm_p, 1024)
            xs = _matres(mix_s.reshape(nb, -1), ev_w_out, xs, e, nb, 2048)
            ret_p.append(st_p)
            ret_s.append(st_s)
            cmv_p.append(vn_p)
            cmv_s.append(vn_s)
        else:
            o = l // 2
            lam_init = 0.8 - 0.6 * math.exp(-0.3 * l)
            qn = da_q_norm[o].reshape(1, DA_DH)
            kn = da_k_norm[o].reshape(1, DA_DH)
            lam_vecs = [a[o].reshape(1, DA_DH) for a in (da_lam_q1, da_lam_k1, da_lam_q2, da_lam_k2)]
            subln = da_subln[o].reshape(1, DA_VDIM)
            tb = seq // bm_p
            q_s, k_s, v_s = _normproj_qkv(xs, mix_norm3, l, od_w_in, o, nb, 1024, F32, qn, kn, c2_s, s2_s, 1)
            v_rep = jnp.repeat(v_s.reshape(nb, DA_HEADS, DA_VDIM), 2, axis=1)
            att_s = _decode_attn(q_s.reshape(nb, 2 * DA_HEADS, DA_DH), k_s.reshape(nb, 2 * DA_HEADS, DA_DH), v_rep,
                                 cache_k, cache_v, page_table, lam_vecs, subln, o, lam_init, pages_per_step=8)
            xs = _matres(att_s.reshape(nb, d), od_w_out, xs, o, nb, 2048)
            xp, xs = lax.optimization_barrier((xp, xs))
            qn_scaled = qn * (DA_DH ** -0.5 * math.log2(math.e))
            q_p, k_p, v_p = _normproj_qkv(xp, mix_norm3, l, od_w_in, o, bm_p, 512, BF16, qn_scaled, kn, c2_p, s2_p, tb)
            att_p = _flash_prompt(q_p.T, k_p, v_p, lam_vecs, subln, batch, seq, lam_init, tq=2048, tk=512)
            xp = _matres(att_p, od_w_out, xp, o, bm_p, 1024)
            kp_l.append(k_p.reshape(batch, seq, 2 * DA_HEADS, DA_DH))
            vp_l.append(v_p.reshape(batch, seq, DA_HEADS, DA_VDIM))
            ks_l.append(k_s.reshape(nb, t_new, 2 * DA_HEADS, DA_DH))
            vs_l.append(v_s.reshape(nb, t_new, DA_HEADS, DA_VDIM))
        xp, xs = ffn_pair(xp, xs, ffn_b_norm3, ffn_b_w1, ffn_b_w3, ffn_b_w2, l)
        if l % 2 == 1 and l + 1 < depth:
            xp, kp_l[-1], vp_l[-1] = lax.optimization_barrier((xp, kp_l[-1], vp_l[-1]))

    return (xp.reshape(batch, seq, d), xs.reshape(nb, t_new, d),
            jnp.stack(ret_p), jnp.stack(ret_s), jnp.stack(cmv_p), jnp.stack(cmv_s),
            jnp.stack(kp_l), jnp.stack(vp_l), jnp.stack(ks_l), jnp.stack(vs_l))
```
